```python
import math
import jax, jax.numpy as jnp
from jax import lax
import numpy as np

D_MODEL = 1024
BATCH = 1
SEQ = 16384
DEPTH = 2
DEC_BATCH = 16
DEC_SEQ = 4096
PAST_LEN = 128

GRID_W = 64
HEAD_DIM = 64
RWKV_HEADS = 8
RWKV_DIM = RWKV_HEADS * HEAD_DIM
ATTN_HEADS = 8
KV_HEADS = 2
ATTN_GROUP = ATTN_HEADS // KV_HEADS
ATTN_DIM = ATTN_HEADS * HEAD_DIM
KV_DIM = KV_HEADS * HEAD_DIM
MIX_DIM = RWKV_DIM + ATTN_DIM
DECAY_LORA = 64
ICLR_LORA = 64
GATE_LORA = 128
N_DIR = 2
CONV_W = 3
D_FF = ((8 * D_MODEL // 3 + 255) // 256) * 256
IN_SPLITS = [3 * RWKV_DIM, ATTN_DIM, KV_DIM, KV_DIM, N_DIR * DECAY_LORA, N_DIR * ICLR_LORA, GATE_LORA]
IN_COLS = sum(IN_SPLITS)
BLOCK_Q = 128
ROPE_THETA = 10000.0
NORM_EPS = 1e-6
QK_EPS = 1e-6
GN_EPS = 64e-5
DECAY_SCALE = math.exp(-0.5)

kernel_name = "hymba_rwkv7_gqa_axialrope_encoder"


def _split_points(sizes):
    return [int(s) for s in np.cumsum(sizes)[:-1]]


def rms_norm(x, g, eps):
    xf = x.astype(jnp.float32)
    y = xf * lax.rsqrt(jnp.mean(xf * xf, axis=-1, keepdims=True) + eps)
    return (y * g.astype(jnp.float32)).astype(x.dtype)


def centred_conv(x, w):
    xp = jnp.pad(x, ((0, 0), (1, 1), (0, 0)))
    return w[0] * xp[:, :-2] + w[1] * xp[:, 1:-1] + w[2] * xp[:, 2:]


def rope_1d(x, pos):
    quarter = x.shape[-1] // 2
    freq = 1.0 / (ROPE_THETA ** (jnp.arange(quarter, dtype=jnp.float32) / quarter))
    ang = pos.astype(jnp.float32)[:, None] * freq[None, :]
    cos = jnp.cos(ang)[None, :, None, :]
    sin = jnp.sin(ang)[None, :, None, :]
    x1, x2 = x[..., :quarter], x[..., quarter:]
    return jnp.concatenate([x1 * cos - x2 * sin, x2 * cos + x1 * sin], axis=-1)


def axial_rope(x, row, col):
    half = HEAD_DIM // 2
    xf = x.astype(jnp.float32)
    out = jnp.concatenate([rope_1d(xf[..., :half], row), rope_1d(xf[..., half:], col)], axis=-1)
    return out.astype(x.dtype)


def block_attention(q, k, v):
    B, T = q.shape[0], q.shape[1]
    nblk = T // BLOCK_Q
    qb = q.reshape(B, nblk, BLOCK_Q, KV_HEADS, ATTN_GROUP, HEAD_DIM).transpose(1, 0, 2, 3, 4, 5)
    kf = k.astype(jnp.float32)
    vf = v.astype(jnp.float32)
    scale = HEAD_DIM ** -0.5

    def one_block(qblk):
        s = jnp.einsum('bqhgd,bshd->bhgqs', qblk.astype(jnp.float32), kf) * scale
        p = jax.nn.softmax(s, axis=-1)
        return jnp.einsum('bhgqs,bshd->bqhgd', p, vf)

    o = lax.map(one_block, qb)
    return o.transpose(1, 0, 2, 3, 4, 5).reshape(B, T, ATTN_DIM)


def rwkv7_scan(r, w, k, v, a, b, reverse):
    B, T, H, N = r.shape
    xs = tuple(jnp.moveaxis(z, 1, 0) for z in (r, w, k, v, a, b))

    def step(S, inp):
        rt, wt, kt, vt, at, bt = inp
        sa = jnp.einsum('bhvk,bhk->bhv', S, at)
        S = S * wt[:, :, None, :] + sa[..., :, None] * bt[..., None, :] + vt[..., :, None] * kt[..., None, :]
        y = jnp.einsum('bhvk,bhk->bhv', S, rt)
        return S, y

    S0 = jnp.zeros((B, H, N, N), jnp.float32)
    _, ys = lax.scan(step, S0, xs, reverse=reverse)
    return jnp.moveaxis(ys, 0, 1)


def hybrid_layer(x, c, ada_w, ada_b, norm_mix_g, norm_ffn_g, w_in, conv_w, decay_w0, decay_up,
                 iclr_a0, iclr_up, gate_up, k_k, k_a, r_k, ln_x_g, ln_x_b, q_norm_g, k_norm_g,
                 w_out, w_ffn_in, w_ffn_out):
    B, T, _ = x.shape
    rows = T // GRID_W
    row = jnp.repeat(jnp.arange(rows, dtype=jnp.int32), GRID_W, total_repeat_length=T)
    col = jnp.tile(jnp.arange(GRID_W, dtype=jnp.int32), rows)
    f32 = jnp.float32

    mod = (jax.nn.silu(c) @ ada_w + ada_b)[:, None, :]
    shift_m, scale_m, gate_m, shift_f, scale_f, gate_f = jnp.split(mod, 6, axis=-1)

    h = rms_norm(x, norm_mix_g, NORM_EPS) * (1.0 + scale_m) + shift_m
    proj = h @ w_in
    rkv, q, ka, va, xw, xa, xg = jnp.split(proj, _split_points(IN_SPLITS), axis=-1)

    r, kr, vr = jnp.split(centred_conv(rkv, conv_w), 3, axis=-1)
    xw = xw.reshape(B, T, N_DIR, DECAY_LORA)
    xa = xa.reshape(B, T, N_DIR, ICLR_LORA)
    decay_logit = decay_w0 + jnp.einsum('btdr,drc->btdc', jnp.tanh(xw), decay_up)
    w = jnp.exp(-DECAY_SCALE * jax.nn.sigmoid(decay_logit.astype(f32)))
    iclr = jax.nn.sigmoid((iclr_a0 + jnp.einsum('btdr,drc->btdc', xa, iclr_up)).astype(f32))
    g = (jax.nn.sigmoid(xg) @ gate_up).astype(f32)

    rf = r.astype(f32)
    kf = kr.astype(f32)
    vf = vr.astype(f32)
    kkh = (kf * k_k.astype(f32)).reshape(B, T, RWKV_HEADS, HEAD_DIM)
    kk = kkh * lax.rsqrt(jnp.sum(kkh * kkh, axis=-1, keepdims=True) + 1e-12)
    k_dir = kf[:, :, None, :] * (1.0 + (iclr - 1.0) * k_a.astype(f32))

    heads = lambda z: z.reshape(B, T, RWKV_HEADS, HEAD_DIM)
    rh, vh = heads(rf), heads(vf)
    y_fwd = rwkv7_scan(rh, heads(w[:, :, 0]), heads(k_dir[:, :, 0]), vh, -kk,
                       kk * heads(iclr[:, :, 0]), reverse=False)
    y_bwd = rwkv7_scan(rh, heads(w[:, :, 1]), heads(k_dir[:, :, 1]), vh, -kk,
                       kk * heads(iclr[:, :, 1]), reverse=True)
    y = y_fwd + y_bwd
    mu = jnp.mean(y, axis=-1, keepdims=True)
    var = jnp.mean(jnp.square(y - mu), axis=-1, keepdims=True)
    yn = ((y - mu) * lax.rsqrt(var + GN_EPS)).reshape(B, T, RWKV_DIM)
    yn = yn * ln_x_g.astype(f32) + ln_x_b.astype(f32)
    k_bonus = heads(0.5 * (k_dir[:, :, 0] + k_dir[:, :, 1]))
    bonus = jnp.sum(rh * k_bonus * r_k.astype(f32), axis=-1, keepdims=True) * vh
    rwkv_out = ((yn + bonus.reshape(B, T, RWKV_DIM)) * g).astype(x.dtype)

    qh = rms_norm(q.reshape(B, T, ATTN_HEADS, HEAD_DIM), q_norm_g, QK_EPS)
    kh = rms_norm(ka.reshape(B, T, KV_HEADS, HEAD_DIM), k_norm_g, QK_EPS)
    vh_a = va.reshape(B, T, KV_HEADS, HEAD_DIM)
    qh = axial_rope(qh, row, col)
    kh = axial_rope(kh, row, col)
    attn_out = block_attention(qh, kh, vh_a).astype(x.dtype)

    mix = jnp.concatenate([rwkv_out, attn_out], axis=-1) @ w_out
    x = x + gate_m * mix

    h2 = rms_norm(x, norm_ffn_g, NORM_EPS) * (1.0 + scale_f) + shift_f
    gt, up = jnp.split(h2 @ w_ffn_in, 2, axis=-1)
    x = x + gate_f * ((jax.nn.silu(gt) * up) @ w_ffn_out)
    return x


def setup_inputs(seed: int = 0) -> dict:
    key = jax.random.key(seed)
    ks = jax.random.split(key, 32)
    nrm = lambda i, shape: jax.random.normal(ks[i], shape, jnp.float32)
    centre = jnp.array([0.0, 1.0, 0.0], jnp.float32)[None, :, None]
    return {
        "x_prompt": nrm(0, (BATCH, SEQ, D_MODEL)),
        "x_sample": nrm(1, (DEC_BATCH, DEC_SEQ, D_MODEL)),
        "c_prompt": nrm(2, (BATCH, D_MODEL)),
        "c_sample": nrm(3, (DEC_BATCH, D_MODEL)),
        "ada_w": nrm(4, (DEPTH, D_MODEL, 6 * D_MODEL)) * (0.5 * D_MODEL ** -0.5),
        "ada_b": nrm(5, (DEPTH, 6 * D_MODEL)) * 0.02,
        "norm_mix_g": 1.0 + 0.02 * nrm(6, (DEPTH, D_MODEL)),
        "norm_ffn_g": 1.0 + 0.02 * nrm(7, (DEPTH, D_MODEL)),
        "w_in": nrm(8, (DEPTH, D_MODEL, IN_COLS)) * D_MODEL ** -0.5,
        "conv_w": centre + 0.2 * nrm(9, (DEPTH, CONV_W, 3 * RWKV_DIM)),
        "decay_w0": 0.5 * nrm(10, (DEPTH, N_DIR, RWKV_DIM)),
        "decay_up": nrm(11, (DEPTH, N_DIR, DECAY_LORA, RWKV_DIM)) * (0.5 * DECAY_LORA ** -0.5),
        "iclr_a0": 0.5 * nrm(12, (DEPTH, N_DIR, RWKV_DIM)),
        "iclr_up": nrm(13, (DEPTH, N_DIR, ICLR_LORA, RWKV_DIM)) * (0.5 * ICLR_LORA ** -0.5),
        "gate_up": nrm(14, (DEPTH, GATE_LORA, RWKV_DIM)) * GATE_LORA ** -0.5,
        "k_k": 0.85 + 0.1 * nrm(15, (DEPTH, RWKV_DIM)),
        "k_a": 1.0 + 0.1 * nrm(16, (DEPTH, RWKV_DIM)),
        "r_k": 0.1 * nrm(17, (DEPTH, RWKV_HEADS, HEAD_DIM)),
        "ln_x_g": 1.0 + 0.02 * nrm(18, (DEPTH, RWKV_DIM)),
        "ln_x_b": 0.02 * nrm(19, (DEPTH, RWKV_DIM)),
        "q_norm_g": 1.0 + 0.02 * nrm(20, (DEPTH, HEAD_DIM)),
        "k_norm_g": 1.0 + 0.02 * nrm(21, (DEPTH, HEAD_DIM)),
        "w_out": nrm(22, (DEPTH, MIX_DIM, D_MODEL)) * MIX_DIM ** -0.5,
        "w_ffn_in": nrm(23, (DEPTH, D_MODEL, 2 * D_FF)) * D_MODEL ** -0.5,
        "w_ffn_out": nrm(24, (DEPTH, D_FF, D_MODEL)) * D_FF ** -0.5,
    }


def reference(x_prompt, x_sample, c_prompt, c_sample, ada_w, ada_b, norm_mix_g, norm_ffn_g, w_in,
              conv_w, decay_w0, decay_up, iclr_a0, iclr_up, gate_up, k_k, k_a, r_k, ln_x_g, ln_x_b,
              q_norm_g, k_norm_g, w_out, w_ffn_in, w_ffn_out):
    def run_trunk(x, c):
        for l in range(DEPTH):
            x = hybrid_layer(x, c, ada_w[l], ada_b[l], norm_mix_g[l], norm_ffn_g[l], w_in[l],
                             conv_w[l], decay_w0[l], decay_up[l], iclr_a0[l], iclr_up[l],
                             gate_up[l], k_k[l], k_a[l], r_k[l], ln_x_g[l], ln_x_b[l],
                             q_norm_g[l], k_norm_g[l], w_out[l], w_ffn_in[l], w_ffn_out[l])
        return x

    y_prompt = run_trunk(x_prompt, c_prompt)
    y_sample = run_trunk(x_sample, c_sample)
    return (y_prompt, y_sample)
```

```python
import functools
import math

import jax
import jax.numpy as jnp
import numpy as np
from jax import lax
from jax.experimental import pallas as pl
from jax.experimental.pallas import tpu as pltpu

f32 = jnp.float32
bf16 = jnp.bfloat16

D_MODEL = 1024
HEAD_DIM = 64
RWKV_DIM = 512
ATTN_DIM = 512
KV_HEADS = 2
LORA = 64
D_FF = 2816
GRID_W = 64
ROPE_THETA = 10000.0
NORM_EPS = 1e-6
QK_EPS = 1e-6
GN_EPS = 64e-5
DECAY_SCALE = math.exp(-0.5)

LANES = 128
VMEM_LIMIT = 48 * 1024 * 1024

ROW_BLOCK = 512
FFN_CHUNK = 1408
CHUNK = 64
SCAN_CHUNKS = 2
ATTN_BQ = 512
ATTN_BK = 1024

C_RKV = 0
C_LORA = 1536
C_Q = 1920
C_K = 2432
C_V = 2688
C_END = 2944


def _cparams(sem):
    return pltpu.CompilerParams(dimension_semantics=sem, vmem_limit_bytes=VMEM_LIMIT)


def _dot(a, b):
    return jnp.dot(a.astype(bf16), b.astype(bf16), preferred_element_type=f32)


def _dot_nt(a, b):
    return lax.dot_general(a.astype(bf16), b.astype(bf16), (((1,), (1,)), ((), ())),
                           preferred_element_type=f32)


def _dot_tn(a, b):
    return lax.dot_general(a.astype(bf16), b.astype(bf16), (((0,), (0,)), ((), ())),
                           preferred_element_type=f32)


def _split2(a):
    hi = a.astype(bf16)
    lo = (a - hi.astype(f32)).astype(bf16)
    return hi, lo


def _dot_exact_rhs(a, g):
    hi, lo = _split2(a)
    return (jnp.dot(hi, g, preferred_element_type=f32) + jnp.dot(lo, g, preferred_element_type=f32))


def _sigmoid(x):
    return 1.0 / (1.0 + jnp.exp(-x))


def _iota(shape, dim):
    return lax.broadcasted_iota(jnp.int32, shape, dim)


def _mod_kernel(c_ref, w_ref, b_ref, o_ref):
    c = c_ref[...]
    s = c * _sigmoid(c)
    sh, sl = _split2(s)
    wh, wl = _split2(w_ref[...])
    acc = jnp.dot(sh, wh, preferred_element_type=f32)
    acc += jnp.dot(sh, wl, preferred_element_type=f32)
    acc += jnp.dot(sl, wh, preferred_element_type=f32)
    o_ref[...] = acc + b_ref[...]


def _modulation(c, ada_w, ada_b):
    B = c.shape[0]
    Bp = max(8, B)
    cp = jnp.pad(c, ((0, Bp - B), (0, 0)))
    n = ada_w.shape[1] // D_MODEL
    out = pl.pallas_call(
        _mod_kernel,
        grid=(n,),
        in_specs=[pl.BlockSpec((Bp, D_MODEL), lambda j: (0, 0)),
                  pl.BlockSpec((D_MODEL, D_MODEL), lambda j: (0, j)),
                  pl.BlockSpec((1, D_MODEL), lambda j: (0, j))],
        out_specs=pl.BlockSpec((Bp, D_MODEL), lambda j: (0, j)),
        out_shape=jax.ShapeDtypeStruct((Bp, n * D_MODEL), f32),
        compiler_params=_cparams(("arbitrary",)),
        name="adaln_mod",
    )(cp, ada_w, ada_b.reshape(1, -1))
    return out[:B].reshape(B, n, D_MODEL)


def _rope(x, cos, sin):
    w = x.shape[1]
    up = pltpu.roll(x, w - 16, 1)
    dn = pltpu.roll(x, 16, 1)
    first = (_iota((1, w), 1) & 16) == 0
    return x * cos + jnp.where(first, up, dn) * sin


def _qk_norm(x, g_ref, gain):
    w = x.shape[1]
    ms = _dot_exact_rhs(x * x, g_ref[0:w, 0:w]) * (1.0 / HEAD_DIM)
    return x * lax.rsqrt(ms + QK_EPS) * gain


def _in_kernel(x_ref, mod_ref, ng_ref, w_ref, cos_ref, sin_ref, qg_ref, kg_ref, gs_ref,
               rkv_ref, lora_ref, q_ref, k_ref, v_ref):
    x = x_ref[0]
    ms = jnp.mean(x * x, axis=-1, keepdims=True)
    h = x * lax.rsqrt(ms + NORM_EPS) * ng_ref[...]
    h = h * (1.0 + mod_ref[0, 1:2, :]) + mod_ref[0, 0:1, :]
    hb = h.astype(bf16)
    rkv_ref[0] = jnp.dot(hb, w_ref[:, C_RKV:C_LORA], preferred_element_type=f32)
    lora_ref[0] = jnp.dot(hb, w_ref[:, C_LORA:C_Q], preferred_element_type=f32)
    q = jnp.dot(hb, w_ref[:, C_Q:C_K], preferred_element_type=f32)
    k = jnp.dot(hb, w_ref[:, C_K:C_V], preferred_element_type=f32)
    v_ref[0] = jnp.dot(hb, w_ref[:, C_V:C_END], preferred_element_type=f32).astype(bf16)
    cos = cos_ref[...]
    sin = sin_ref[...]
    cos4 = jnp.concatenate([cos] * 4, axis=1)
    sin4 = jnp.concatenate([sin] * 4, axis=1)
    qn = _rope(_qk_norm(q, gs_ref, qg_ref[...]), cos4, sin4)
    q_ref[0] = (qn * (HEAD_DIM ** -0.5)).astype(bf16)
    kn = _rope(_qk_norm(k, gs_ref, kg_ref[...]), cos4[:, 0:256], sin4[:, 0:256])
    k_ref[0] = kn.astype(bf16)


def _in_proj(x, mod, norm_g, w_all, cos, sin, qg, kg, gsum):
    B, T, _ = x.shape
    bm = ROW_BLOCK
    row3 = lambda b, i: (b, i, 0)
    const2 = lambda b, i: (0, 0)
    return pl.pallas_call(
        _in_kernel,
        grid=(B, T // bm),
        in_specs=[pl.BlockSpec((1, bm, D_MODEL), row3),
                  pl.BlockSpec((1, 6, D_MODEL), lambda b, i: (b, 0, 0)),
                  pl.BlockSpec((1, D_MODEL), const2),
                  pl.BlockSpec((D_MODEL, C_END), const2),
                  pl.BlockSpec((bm, LANES), lambda b, i: (i, 0)),
                  pl.BlockSpec((bm, LANES), lambda b, i: (i, 0)),
                  pl.BlockSpec((1, 512), const2),
                  pl.BlockSpec((1, 256), const2),
                  pl.BlockSpec((512, 512), const2)],
        out_specs=[pl.BlockSpec((1, bm, 1536), row3),
                   pl.BlockSpec((1, bm, 384), row3),
                   pl.BlockSpec((1, bm, 512), row3),
                   pl.BlockSpec((1, bm, 256), row3),
                   pl.BlockSpec((1, bm, 256), row3)],
        out_shape=[jax.ShapeDtypeStruct((B, T, 1536), f32),
                   jax.ShapeDtypeStruct((B, T, 384), f32),
                   jax.ShapeDtypeStruct((B, T, 512), bf16),
                   jax.ShapeDtypeStruct((B, T, 256), bf16),
                   jax.ShapeDtypeStruct((B, T, 256), bf16)],
        compiler_params=_cparams(("parallel", "parallel")),
        name="in_proj",
    )(x, mod, norm_g, w_all, cos, sin, qg, kg, gsum)


def _prep_kernel(cur_ref, prev_ref, next_ref, lora_ref, cw_ref, wd_ref, w0_ref, wi_ref, a0_ref,
                 kk_ref, gs_ref, r_ref, k_ref, v_ref, kkn_ref, lw_ref, ic_ref):
    i = pl.program_id(1)
    n = pl.num_programs(1)
    bm = cur_ref.shape[1]
    rows = _iota((bm, 1), 0)
    has_prev = jnp.where(i > 0, 1.0, 0.0)
    has_next = jnp.where(i < n - 1, 1.0, 0.0)
    outs = (r_ref, k_ref, v_ref)
    for p in range(3):
        cols = slice(p * 512, (p + 1) * 512)
        cur = cur_ref[0, :, cols]
        prev_row = prev_ref[0, 7:8, cols] * has_prev
        next_row = next_ref[0, 0:1, cols] * has_next
        xm1 = jnp.where(rows == 0, prev_row, pltpu.roll(cur, 1, 0))
        xp1 = jnp.where(rows == bm - 1, next_row, pltpu.roll(cur, bm - 1, 0))
        y = cw_ref[0:1, cols] * xm1 + cw_ref[1:2, cols] * cur + cw_ref[2:3, cols] * xp1
        outs[p][0] = y
        if p == 1:
            kkh = y * kk_ref[...]
            ss = _dot_exact_rhs(kkh * kkh, gs_ref[...])
            kkn_ref[0] = kkh * lax.rsqrt(ss + 1e-12)
    xw = lora_ref[0, :, 0:128]
    xa = lora_ref[0, :, 128:256]
    dl = _dot(jnp.tanh(xw), wd_ref[...]) + w0_ref[...]
    lw = -DECAY_SCALE * _sigmoid(dl)
    lw_ref[0, 0] = lw[:, 0:512]
    lw_ref[1, 0] = lw[:, 512:1024]
    ic = _sigmoid(_dot(xa, wi_ref[...]) + a0_ref[...])
    ic_ref[0, 0] = ic[:, 0:512]
    ic_ref[1, 0] = ic[:, 512:1024]


def _rwkv_prep(rkv, lora, conv_w, wd, w0, wi, a0, k_k, gsum):
    B, T, _ = rkv.shape
    bm = ROW_BLOCK
    hb = bm // 8
    nh = T // 8
    row3 = lambda b, i: (b, i, 0)
    const2 = lambda b, i: (0, 0)
    o512 = jax.ShapeDtypeStruct((B, T, 512), f32)
    o2 = jax.ShapeDtypeStruct((2, B, T, 512), f32)
    return pl.pallas_call(
        _prep_kernel,
        grid=(B, T // bm),
        in_specs=[pl.BlockSpec((1, bm, 1536), row3),
                  pl.BlockSpec((1, 8, 1536), lambda b, i: (b, jnp.maximum(i * hb - 1, 0), 0)),
                  pl.BlockSpec((1, 8, 1536), lambda b, i: (b, jnp.minimum((i + 1) * hb, nh - 1), 0)),
                  pl.BlockSpec((1, bm, 384), row3),
                  pl.BlockSpec((3, 1536), const2),
                  pl.BlockSpec((128, 1024), const2),
                  pl.BlockSpec((1, 1024), const2),
                  pl.BlockSpec((128, 1024), const2),
                  pl.BlockSpec((1, 1024), const2),
                  pl.BlockSpec((1, 512), const2),
                  pl.BlockSpec((512, 512), const2)],
        out_specs=[pl.BlockSpec((1, bm, 512), row3)] * 4
        + [pl.BlockSpec((2, 1, bm, 512), lambda b, i: (0, b, i, 0))] * 2,
        out_shape=[o512, o512, o512, o512, o2, o2],
        compiler_params=_cparams(("parallel", "parallel")),
        name="rwkv_prep",
    )(rkv, rkv, rkv, lora, conv_w, wd, w0, wi, a0, k_k, gsum)


def _scan_kernel(r_ref, k_ref, v_ref, kk_ref, lw_ref, ic_ref, ka_ref, y_ref, h_ref, *, nc):
    L = CHUNK
    d = pl.program_id(0)
    fwd = d == 0

    @pl.when(pl.program_id(2) == 0)
    def _():
        h_ref[...] = jnp.zeros_like(h_ref)

    sgn = jnp.where(fwd, 1, -1)
    order = (_iota((L, LANES), 1) - _iota((L, LANES), 0)) * sgn
    left = _iota((L, LANES), 1) < L
    strict_p = (order < 0) & left
    incl_p = (order <= 0) & left
    eye_r = jnp.where(_iota((L, LANES), 1) - L == _iota((L, LANES), 0), 1.0, 0.0)
    right_f = jnp.where(left, 0.0, 1.0)
    incl_bf = jnp.where(order[:, 0:L] <= 0, 1.0, 0.0).astype(bf16)
    row2 = _iota((2 * L, L), 0)
    order2 = (_iota((2 * L, L), 1) - (row2 & (L - 1))) * sgn
    mask_kr = order2 < jnp.where(row2 >= L, 1, 0)
    lane = _iota((1, LANES), 1)
    head_m = (jnp.where(lane < L, 1.0, 0.0), jnp.where(lane < L, 0.0, 1.0))
    blockdiag = (_iota((LANES, LANES), 0) < L) == (_iota((LANES, LANES), 1) < L)
    eye128 = _iota((LANES, LANES), 0) == _iota((LANES, LANES), 1)
    z128 = jnp.zeros((L, LANES), f32)
    z256 = jnp.zeros((L, 2 * LANES), f32)
    ka = ka_ref[...]

    hs = [h_ref[p] for p in range(4)]
    for i in range(nc):
        ci = jnp.where(fwd, i, nc - 1 - i)
        rows = pl.ds(pl.multiple_of(ci * L, L), L)
        lw = lw_ref[0, 0, rows, :]
        ic = ic_ref[0, 0, rows, :]
        r = r_ref[0, rows, :]
        v = v_ref[0, rows, :]
        kk = kk_ref[0, rows, :]
        kd = k_ref[0, rows, :] * (1.0 + (ic - 1.0) * ka)
        b = kk * ic
        l1 = lw.astype(bf16)
        e1 = lw - l1.astype(f32)
        l2 = e1.astype(bf16)
        l3 = (e1 - l2.astype(f32)).astype(bf16)
        cum = (jnp.dot(incl_bf, l1, preferred_element_type=f32)
               + jnp.dot(incl_bf, l2, preferred_element_type=f32)
               + jnp.dot(incl_bf, l3, preferred_element_type=f32))
        tot = jnp.where(fwd, cum[L - 1:L, :], cum[0:1, :])
        p_inv = jnp.exp(-cum)
        p_end = jnp.exp(tot - cum)
        p_tot = jnp.exp(tot)
        rt = r * jnp.exp(cum)
        at = -kk * jnp.exp(cum - lw)
        bt = b * p_inv
        kt = kd * p_inv
        bh = b * p_end
        kh = kd * p_end
        for p in range(4):
            ps = slice(p * LANES, (p + 1) * LANES)
            at_p, rt_p, v_p = at[:, ps], rt[:, ps], v[:, ps]
            bpad = jnp.concatenate([bt[:, ps], z128], axis=0).astype(bf16)
            kt_b = kt[:, ps].astype(bf16)
            a_pair = z128
            u_pair = z128
            r_pair = rt_p
            y_pair = z128
            for hh in range(2):
                m = head_m[hh]
                am = at_p * m
                xm = jnp.concatenate([am, rt_p * m], axis=0).astype(bf16)
                gb = _dot_nt(xm, bpad)
                gk = _dot_nt(xm, kt_b)
                pk = jnp.where(strict_p, gb[0:L], 0.0) + eye_r
                for _ in range(6):
                    prod = _dot(pk, jnp.concatenate([pk, z128], axis=0))
                    pk = prod + pk * right_f
                a_rb = jnp.where(incl_p, gb[L:2 * L], 0.0)
                akrk = jnp.where(mask_kr, gk, 0.0)
                w = _dot(akrk, v_p * m)
                tu = _dot(pk, jnp.concatenate([z256, jnp.concatenate([am, w[0:L]], axis=1)], axis=0))
                ry = _dot(a_rb, jnp.concatenate([tu, z256], axis=0))
                a_pair = a_pair + tu[:, 0:LANES]
                u_pair = u_pair + tu[:, LANES:]
                r_pair = r_pair + ry[:, 0:LANES]
                y_pair = y_pair + ry[:, LANES:] + w[L:2 * L]
            zl = jnp.concatenate([bh[:, ps], kh[:, ps]], axis=0)
            zr = jnp.concatenate([jnp.concatenate([a_pair, u_pair], axis=1),
                                  jnp.concatenate([z128, v_p], axis=1)], axis=0)
            mc = _dot_tn(zl, zr)
            m_p = jnp.where(blockdiag, mc[:, 0:LANES], 0.0)
            c_p = jnp.where(blockdiag, mc[:, LANES:], 0.0)
            p_col = jnp.sum(jnp.where(eye128, p_tot[:, ps], 0.0), axis=1, keepdims=True)
            sd = _dot(jnp.concatenate([r_pair, m_p], axis=0), hs[p])
            y_ref[0, 0, rows, ps] = sd[0:L] + y_pair
            hs[p] = p_col * hs[p] + sd[L:] + c_p
    for p in range(4):
        h_ref[p] = hs[p]


def _rwkv_scan(r, k, v, kk, lw, ic, k_a):
    B, T, _ = r.shape
    nc = SCAN_CHUNKS
    lb = nc * CHUNK
    ns = T // lb

    def blk(d, b, s):
        return jnp.where(d == 0, s, ns - 1 - s)

    shared = pl.BlockSpec((1, lb, 512), lambda d, b, s: (b, blk(d, b, s), 0))
    perdir = pl.BlockSpec((1, 1, lb, 512), lambda d, b, s: (d, b, blk(d, b, s), 0))
    return pl.pallas_call(
        functools.partial(_scan_kernel, nc=nc),
        grid=(2, B, ns),
        in_specs=[shared, shared, shared, shared, perdir, perdir,
                  pl.BlockSpec((1, 512), lambda d, b, s: (0, 0))],
        out_specs=perdir,
        out_shape=jax.ShapeDtypeStruct((2, B, T, 512), f32),
        scratch_shapes=[pltpu.VMEM((4, LANES, LANES), f32)],
        compiler_params=_cparams(("arbitrary", "arbitrary", "arbitrary")),
        name="rwkv_scan",
    )(r, k, v, kk, lw, ic, k_a)


def _attn_kernel(q_ref, k_ref, v_ref, o_ref, m_ref, l_ref, acc_ref):
    ki = pl.program_id(3)

    @pl.when(ki == 0)
    def _():
        m_ref[...] = jnp.full_like(m_ref, -1e30)
        l_ref[...] = jnp.zeros_like(l_ref)
        acc_ref[...] = jnp.zeros_like(acc_ref)

    k = k_ref[0]
    v = v_ref[0]
    lane0 = _iota((1, LANES), 1) < HEAD_DIM
    zero = jnp.zeros_like(k)
    km = (jnp.where(lane0, k, zero), jnp.where(lane0, zero, k))
    vm = (jnp.where(lane0, v, zero), jnp.where(lane0, zero, v))
    for j in range(2):
        qs = q_ref[0, :, j * LANES:(j + 1) * LANES]
        alphas = []
        pv = None
        for hh in range(2):
            idx = 2 * j + hh
            s = lax.dot_general(qs, km[hh], (((1,), (1,)), ((), ())), preferred_element_type=f32)
            m_old = m_ref[idx]
            m_new = jnp.maximum(m_old, jnp.max(s, axis=1, keepdims=True))
            alpha = jnp.exp(m_old - m_new)
            p = jnp.exp(s - m_new)
            l_ref[idx] = alpha * l_ref[idx] + jnp.sum(p, axis=1, keepdims=True)
            m_ref[idx] = m_new
            alphas.append(alpha)
            t = jnp.dot(p.astype(bf16), vm[hh], preferred_element_type=f32)
            pv = t if pv is None else pv + t
        acc_ref[j] = acc_ref[j] * jnp.where(lane0, alphas[0], alphas[1]) + pv

    @pl.when(ki == pl.num_programs(3) - 1)
    def _():
        for j in range(2):
            inv = jnp.where(lane0, 1.0 / l_ref[2 * j], 1.0 / l_ref[2 * j + 1])
            o_ref[0, :, j * LANES:(j + 1) * LANES] = (acc_ref[j] * inv).astype(o_ref.dtype)


def _attention(q, k, v):
    B, T, _ = q.shape
    bq, bk = ATTN_BQ, ATTN_BK
    return pl.pallas_call(
        _attn_kernel,
        grid=(B, KV_HEADS, T // bq, T // bk),
        in_specs=[pl.BlockSpec((1, bq, 256), lambda b, g, qi, ki: (b, qi, g)),
                  pl.BlockSpec((1, bk, LANES), lambda b, g, qi, ki: (b, ki, g)),
                  pl.BlockSpec((1, bk, LANES), lambda b, g, qi, ki: (b, ki, g))],
        out_specs=pl.BlockSpec((1, bq, 256), lambda b, g, qi, ki: (b, qi, g)),
        out_shape=jax.ShapeDtypeStruct((B, T, ATTN_DIM), bf16),
        scratch_shapes=[pltpu.VMEM((4, bq, 1), f32), pltpu.VMEM((4, bq, 1), f32),
                        pltpu.VMEM((2, bq, LANES), f32)],
        compiler_params=_cparams(("parallel", "parallel", "parallel", "arbitrary")),
        name="gqa_attention",
    )(q, k, v)


def _mix_kernel(x_ref, y_ref, r_ref, k_ref, v_ref, ic_ref, lora_ref, attn_ref, mod_ref, gs_ref,
                gup_ref, wo_ref, lng_ref, lnb_ref, rk_ref, ka_ref, o_ref):
    gs = gs_ref[...]
    y = y_ref[0, 0] + y_ref[1, 0]
    mu = _dot_exact_rhs(y, gs) * (1.0 / HEAD_DIM)
    yc = y - mu
    var = _dot_exact_rhs(yc * yc, gs) * (1.0 / HEAD_DIM)
    yn = yc * lax.rsqrt(var + GN_EPS) * lng_ref[...] + lnb_ref[...]
    icm = 0.5 * (ic_ref[0, 0] + ic_ref[1, 0])
    r = r_ref[0]
    kb = k_ref[0] * (1.0 + (icm - 1.0) * ka_ref[...])
    bonus = _dot_exact_rhs(r * kb * rk_ref[...], gs) * v_ref[0]
    g = _dot(_sigmoid(lora_ref[0, :, 256:384]), gup_ref[...])
    rw = ((yn + bonus) * g).astype(bf16)
    mix = (jnp.dot(rw, wo_ref[0:RWKV_DIM, :], preferred_element_type=f32)
           + jnp.dot(attn_ref[0], wo_ref[RWKV_DIM:, :], preferred_element_type=f32))
    o_ref[0] = x_ref[0] + mod_ref[0, 2:3, :] * mix


def _mix(x, y, r, k, v, ic, lora, attn, mod, gsum, gate_up, w_out, ln_g, ln_b, r_k, k_a):
    B, T, _ = x.shape
    bm = ROW_BLOCK
    row3 = lambda b, i: (b, i, 0)
    dir4 = lambda b, i: (0, b, i, 0)
    const2 = lambda b, i: (0, 0)
    vec512 = pl.BlockSpec((1, 512), const2)
    return pl.pallas_call(
        _mix_kernel,
        grid=(B, T // bm),
        in_specs=[pl.BlockSpec((1, bm, D_MODEL), row3),
                  pl.BlockSpec((2, 1, bm, 512), dir4),
                  pl.BlockSpec((1, bm, 512), row3),
                  pl.BlockSpec((1, bm, 512), row3),
                  pl.BlockSpec((1, bm, 512), row3),
                  pl.BlockSpec((2, 1, bm, 512), dir4),
                  pl.BlockSpec((1, bm, 384), row3),
                  pl.BlockSpec((1, bm, 512), row3),
                  pl.BlockSpec((1, 6, D_MODEL), lambda b, i: (b, 0, 0)),
                  pl.BlockSpec((512, 512), const2),
                  pl.BlockSpec((128, 512), const2),
                  pl.BlockSpec((D_MODEL, D_MODEL), const2),
                  vec512, vec512, vec512, vec512],
        out_specs=pl.BlockSpec((1, bm, D_MODEL), row3),
        out_shape=jax.ShapeDtypeStruct((B, T, D_MODEL), f32),
        compiler_params=_cparams(("parallel", "parallel")),
        name="mix_out",
    )(x, y, r, k, v, ic, lora, attn, mod, gsum, gate_up, w_out, ln_g, ln_b, r_k, k_a)


def _ffn_kernel(x_ref, mod_ref, ng_ref, wg_ref, wu_ref, wo_ref, o_ref, h_ref, acc_ref):
    c = pl.program_id(2)

    @pl.when(c == 0)
    def _():
        x = x_ref[0]
        ms = jnp.mean(x * x, axis=-1, keepdims=True)
        h = x * lax.rsqrt(ms + NORM_EPS) * ng_ref[...]
        h_ref[...] = (h * (1.0 + mod_ref[0, 4:5, :]) + mod_ref[0, 3:4, :]).astype(bf16)
        acc_ref[...] = jnp.zeros_like(acc_ref)

    h = h_ref[...]
    gt = jnp.dot(h, wg_ref[...], preferred_element_type=f32)
    up = jnp.dot(h, wu_ref[...], preferred_element_type=f32)
    act = (gt * _sigmoid(gt) * up).astype(bf16)
    acc_ref[...] += jnp.dot(act, wo_ref[...], preferred_element_type=f32)

    @pl.when(c == pl.num_programs(2) - 1)
    def _():
        o_ref[0] = x_ref[0] + mod_ref[0, 5:6, :] * acc_ref[...]


def _ffn(x, mod, norm_g, w_in, w_out):
    B, T, _ = x.shape
    bm = ROW_BLOCK
    ncf = D_FF // FFN_CHUNK
    row3 = lambda b, i, c: (b, i, 0)
    return pl.pallas_call(
        _ffn_kernel,
        grid=(B, T // bm, ncf),
        in_specs=[pl.BlockSpec((1, bm, D_MODEL), row3),
                  pl.BlockSpec((1, 6, D_MODEL), lambda b, i, c: (b, 0, 0)),
                  pl.BlockSpec((1, D_MODEL), lambda b, i, c: (0, 0)),
                  pl.BlockSpec((D_MODEL, FFN_CHUNK), lambda b, i, c: (0, c)),
                  pl.BlockSpec((D_MODEL, FFN_CHUNK), lambda b, i, c: (0, ncf + c)),
                  pl.BlockSpec((FFN_CHUNK, D_MODEL), lambda b, i, c: (c, 0))],
        out_specs=pl.BlockSpec((1, bm, D_MODEL), row3),
        out_shape=jax.ShapeDtypeStruct((B, T, D_MODEL), f32),
        scratch_shapes=[pltpu.VMEM((bm, D_MODEL), bf16), pltpu.VMEM((bm, D_MODEL), f32)],
        compiler_params=_cparams(("parallel", "parallel", "arbitrary")),
        name="ffn",
    )(x, mod, norm_g, w_in, w_in, w_out)


def _rope_tables(T):
    pos = jnp.arange(T, dtype=jnp.int32)
    row = (pos // GRID_W).astype(f32)
    col = (pos % GRID_W).astype(f32)
    quarter = HEAD_DIM // 4
    freq = 1.0 / (ROPE_THETA ** (jnp.arange(quarter, dtype=f32) / quarter))
    ang_r = row[:, None] * freq[None, :]
    ang_c = col[:, None] * freq[None, :]
    cos = jnp.concatenate([jnp.cos(ang_r)] * 2 + [jnp.cos(ang_c)] * 2, axis=1)
    sin = jnp.concatenate([-jnp.sin(ang_r), jnp.sin(ang_r), -jnp.sin(ang_c), jnp.sin(ang_c)], axis=1)
    return jnp.concatenate([cos, cos], axis=1), jnp.concatenate([sin, sin], axis=1)


def _block_lora(up):
    z = jnp.zeros_like(up[0])
    return jnp.concatenate([jnp.concatenate([up[0], z], axis=1),
                            jnp.concatenate([z, up[1]], axis=1)], axis=0)


def _layer_params(l, ada_w, ada_b, norm_mix_g, norm_ffn_g, w_in, conv_w, decay_w0, decay_up, iclr_a0,
                  iclr_up, gate_up, k_k, k_a, r_k, ln_x_g, ln_x_b, q_norm_g, k_norm_g, w_out,
                  w_ffn_in, w_ffn_out):
    wi = w_in[l]
    wk = wi[:, 2048:2176]
    wv = wi[:, 2176:2304]
    dup = lambda w: jnp.concatenate([w[:, 0:64], w[:, 0:64], w[:, 64:128], w[:, 64:128]], axis=1)
    w_all = jnp.concatenate([wi[:, 0:1536], wi[:, 2304:2688], wi[:, 1536:2048], dup(wk), dup(wv)],
                            axis=1).astype(bf16)
    return dict(
        ada_w=ada_w[l], ada_b=ada_b[l],
        norm_mix_g=norm_mix_g[l].reshape(1, -1), norm_ffn_g=norm_ffn_g[l].reshape(1, -1),
        w_all=w_all, conv_w=conv_w[l],
        wd=_block_lora(decay_up[l]).astype(bf16), w0=decay_w0[l].reshape(1, -1),
        wi=_block_lora(iclr_up[l]).astype(bf16), a0=iclr_a0[l].reshape(1, -1),
        gate_up=gate_up[l].astype(bf16),
        k_k=k_k[l].reshape(1, -1), k_a=k_a[l].reshape(1, -1), r_k=r_k[l].reshape(1, -1),
        ln_g=ln_x_g[l].reshape(1, -1), ln_b=ln_x_b[l].reshape(1, -1),
        qg=jnp.tile(q_norm_g[l], 8).reshape(1, -1), kg=jnp.tile(k_norm_g[l], 4).reshape(1, -1),
        w_out=w_out[l].astype(bf16), w_ffn_in=w_ffn_in[l].astype(bf16),
        w_ffn_out=w_ffn_out[l].astype(bf16))


def _layer(x, c, p, cos, sin, gsum):
    mod = _modulation(c, p["ada_w"], p["ada_b"])
    rkv, lora, q, k, v = _in_proj(x, mod, p["norm_mix_g"], p["w_all"], cos, sin, p["qg"], p["kg"], gsum)
    r, kr, vr, kk, lw, ic = _rwkv_prep(rkv, lora, p["conv_w"], p["wd"], p["w0"], p["wi"], p["a0"],
                                       p["k_k"], gsum)
    y = _rwkv_scan(r, kr, vr, kk, lw, ic, p["k_a"])
    attn = _attention(q, k, v)
    x1 = _mix(x, y, r, kr, vr, ic, lora, attn, mod, gsum, p["gate_up"], p["w_out"], p["ln_g"],
              p["ln_b"], p["r_k"], p["k_a"])
    return _ffn(x1, mod, p["norm_ffn_g"], p["w_ffn_in"], p["w_ffn_out"])


def kernel(x_prompt, x_sample, c_prompt, c_sample, ada_w, ada_b, norm_mix_g, norm_ffn_g, w_in, conv_w,
           decay_w0, decay_up, iclr_a0, iclr_up, gate_up, k_k, k_a, r_k, ln_x_g, ln_x_b, q_norm_g,
           k_norm_g, w_out, w_ffn_in, w_ffn_out):
    depth = ada_w.shape[0]
    params = [_layer_params(l, ada_w, ada_b, norm_mix_g, norm_ffn_g, w_in, conv_w, decay_w0, decay_up,
                            iclr_a0, iclr_up, gate_up, k_k, k_a, r_k, ln_x_g, ln_x_b, q_norm_g,
                            k_norm_g, w_out, w_ffn_in, w_ffn_out) for l in range(depth)]
    head = np.arange(512) // HEAD_DIM
    gsum = jnp.asarray(head[:, None] == head[None, :], dtype=bf16)

    def run_trunk(x, c):
        cos, sin = _rope_tables(x.shape[1])
        for p in params:
            x = _layer(x, c, p, cos, sin, gsum)
        return x

    return (run_trunk(x_prompt, c_prompt), run_trunk(x_sample, c_sample))
```

```python
import functools
import math

import jax
import jax.numpy as jnp
import numpy as np
from jax import lax
from jax.experimental import pallas as pl
from jax.experimental.pallas import tpu as pltpu

f32 = jnp.float32
bf16 = jnp.bfloat16

D_MODEL = 1024
HEAD_DIM = 64
RWKV_DIM = 512
ATTN_DIM = 512
KV_HEADS = 2
LORA = 64
D_FF = 2816
GRID_W = 64
ROPE_THETA = 10000.0
NORM_EPS = 1e-6
QK_EPS = 1e-6
GN_EPS = 64e-5
DECAY_SCALE = math.exp(-0.5)

LANES = 128
VMEM_LIMIT = 48 * 1024 * 1024

ROW_BLOCK = 512
FFN_CHUNK = 1408
CHUNK = 64
SCAN_CHUNKS = 4
ATTN_BQ = 1024
ATTN_BK = 1024
ATTN_STRIP = 16
LOG2E = 1.4426950408889634

C_RKV = 0
C_LORA = 1536
C_Q = 1920
C_K = 2432
C_V = 2944
C_END = 3456


def _cparams(sem):
    return pltpu.CompilerParams(dimension_semantics=sem, vmem_limit_bytes=VMEM_LIMIT)


def _dot(a, b):
    return jnp.dot(a.astype(bf16), b.astype(bf16), preferred_element_type=f32)


def _dot_nt(a, b):
    return lax.dot_general(a.astype(bf16), b.astype(bf16), (((1,), (1,)), ((), ())),
                           preferred_element_type=f32)


def _dot_tn(a, b):
    return lax.dot_general(a.astype(bf16), b.astype(bf16), (((0,), (0,)), ((), ())),
                           preferred_element_type=f32)


def _split2(a):
    hi = a.astype(bf16)
    lo = (a - hi.astype(f32)).astype(bf16)
    return hi, lo


def _dot_exact_rhs(a, g):
    hi, lo = _split2(a)
    return (jnp.dot(hi, g, preferred_element_type=f32) + jnp.dot(lo, g, preferred_element_type=f32))


def _sigmoid(x):
    return 1.0 / (1.0 + jnp.exp(-x))


def _iota(shape, dim):
    return lax.broadcasted_iota(jnp.int32, shape, dim)


def _mod_kernel(c_ref, w_ref, b_ref, o_ref):
    c = c_ref[...]
    s = c * _sigmoid(c)
    sh, sl = _split2(s)
    wh, wl = _split2(w_ref[...])
    acc = jnp.dot(sh, wh, preferred_element_type=f32)
    acc += jnp.dot(sh, wl, preferred_element_type=f32)
    acc += jnp.dot(sl, wh, preferred_element_type=f32)
    o_ref[...] = acc + b_ref[...]


def _modulation(c, ada_w, ada_b):
    B = c.shape[0]
    Bp = max(8, B)
    cp = jnp.pad(c, ((0, Bp - B), (0, 0)))
    n = ada_w.shape[1] // D_MODEL
    out = pl.pallas_call(
        _mod_kernel,
        grid=(n,),
        in_specs=[pl.BlockSpec((Bp, D_MODEL), lambda j: (0, 0)),
                  pl.BlockSpec((D_MODEL, D_MODEL), lambda j: (0, j)),
                  pl.BlockSpec((1, D_MODEL), lambda j: (0, j))],
        out_specs=pl.BlockSpec((Bp, D_MODEL), lambda j: (0, j)),
        out_shape=jax.ShapeDtypeStruct((Bp, n * D_MODEL), f32),
        compiler_params=_cparams(("arbitrary",)),
        name="adaln_mod",
    )(cp, ada_w, ada_b.reshape(1, -1))
    return out[:B].reshape(B, n, D_MODEL)


def _rope(x, cos, sin):
    w = x.shape[1]
    up = pltpu.roll(x, w - 16, 1)
    dn = pltpu.roll(x, 16, 1)
    first = (_iota((1, w), 1) & 16) == 0
    return x * cos + jnp.where(first, up, dn) * sin


def _qk_norm(x, g_ref, gain):
    w = x.shape[1]
    ms = _dot_exact_rhs(x * x, g_ref[0:w, 0:w]) * (1.0 / HEAD_DIM)
    return x * lax.rsqrt(ms + QK_EPS) * gain


def _in_kernel(x_ref, mod_ref, ng_ref, w_ref, cos_ref, sin_ref, qg_ref, kg_ref, gs_ref,
               rkv_ref, lora_ref, q_ref, k_ref, v_ref, vs_ref):
    x = x_ref[0]
    ms = jnp.mean(x * x, axis=-1, keepdims=True)
    h = x * lax.rsqrt(ms + NORM_EPS) * ng_ref[...]
    h = h * (1.0 + mod_ref[0, 1:2, :]) + mod_ref[0, 0:1, :]
    hb = h.astype(bf16)
    rkv_ref[0] = jnp.dot(hb, w_ref[:, C_RKV:C_LORA], preferred_element_type=f32)
    lora_ref[0] = jnp.dot(hb, w_ref[:, C_LORA:C_Q], preferred_element_type=f32)
    q = jnp.dot(hb, w_ref[:, C_Q:C_K], preferred_element_type=f32)
    k = jnp.dot(hb, w_ref[:, C_K:C_V], preferred_element_type=f32)
    vs_ref[...] = jnp.dot(hb, w_ref[:, C_V:C_END], preferred_element_type=f32)
    vrow = _iota((ATTN_DIM, 1), 0)
    ones_row = (vrow == 64) | (vrow == 128) | (vrow == 320) | (vrow == 384)
    v_ref[0] = jnp.where(ones_row, 1.0, vs_ref[...].T).astype(bf16)
    cos = cos_ref[...]
    sin = sin_ref[...]
    cos4 = jnp.concatenate([cos] * 4, axis=1)
    sin4 = jnp.concatenate([sin] * 4, axis=1)
    qn = _rope(_qk_norm(q, gs_ref, qg_ref[...]), cos4, sin4)
    q_ref[0] = (qn * (LOG2E * HEAD_DIM ** -0.5)).astype(bf16)
    kn = _rope(_qk_norm(k, gs_ref, kg_ref[...]), cos4, sin4)
    k_ref[0] = kn.astype(bf16)


def _in_proj(x, mod, norm_g, w_all, cos, sin, qg, kg, gsum):
    B, T, _ = x.shape
    bm = ROW_BLOCK
    row3 = lambda b, i: (b, i, 0)
    const2 = lambda b, i: (0, 0)
    return pl.pallas_call(
        _in_kernel,
        grid=(B, T // bm),
        in_specs=[pl.BlockSpec((1, bm, D_MODEL), row3),
                  pl.BlockSpec((1, 6, D_MODEL), lambda b, i: (b, 0, 0)),
                  pl.BlockSpec((1, D_MODEL), const2),
                  pl.BlockSpec((D_MODEL, C_END), const2),
                  pl.BlockSpec((bm, LANES), lambda b, i: (i, 0)),
                  pl.BlockSpec((bm, LANES), lambda b, i: (i, 0)),
                  pl.BlockSpec((1, 512), const2),
                  pl.BlockSpec((1, 512), const2),
                  pl.BlockSpec((512, 512), const2)],
        out_specs=[pl.BlockSpec((1, bm, 1536), row3),
                   pl.BlockSpec((1, bm, 384), row3),
                   pl.BlockSpec((1, bm, 512), row3),
                   pl.BlockSpec((1, bm, 512), row3),
                   pl.BlockSpec((1, 512, bm), lambda b, i: (b, 0, i))],
        out_shape=[jax.ShapeDtypeStruct((B, T, 1536), f32),
                   jax.ShapeDtypeStruct((B, T, 384), f32),
                   jax.ShapeDtypeStruct((B, T, 512), bf16),
                   jax.ShapeDtypeStruct((B, T, 512), bf16),
                   jax.ShapeDtypeStruct((B, 512, T), bf16)],
        scratch_shapes=[pltpu.VMEM((bm, 512), f32)],
        compiler_params=_cparams(("parallel", "parallel")),
        name="in_proj",
    )(x, mod, norm_g, w_all, cos, sin, qg, kg, gsum)


def _prep_kernel(cur_ref, prev_ref, next_ref, lora_ref, cw_ref, wd_ref, w0_ref, wi_ref, a0_ref,
                 kk_ref, gs_ref, r_ref, k_ref, v_ref, kkn_ref, lw_ref, ic_ref):
    i = pl.program_id(1)
    n = pl.num_programs(1)
    bm = cur_ref.shape[1]
    rows = _iota((bm, 1), 0)
    has_prev = jnp.where(i > 0, 1.0, 0.0)
    has_next = jnp.where(i < n - 1, 1.0, 0.0)
    outs = (r_ref, k_ref, v_ref)
    for p in range(3):
        cols = slice(p * 512, (p + 1) * 512)
        cur = cur_ref[0, :, cols]
        prev_row = prev_ref[0, 7:8, cols] * has_prev
        next_row = next_ref[0, 0:1, cols] * has_next
        xm1 = jnp.where(rows == 0, prev_row, pltpu.roll(cur, 1, 0))
        xp1 = jnp.where(rows == bm - 1, next_row, pltpu.roll(cur, bm - 1, 0))
        y = cw_ref[0:1, cols] * xm1 + cw_ref[1:2, cols] * cur + cw_ref[2:3, cols] * xp1
        outs[p][0] = y
        if p == 1:
            kkh = y * kk_ref[...]
            ss = _dot_exact_rhs(kkh * kkh, gs_ref[...])
            kkn_ref[0] = kkh * lax.rsqrt(ss + 1e-12)
    xw = lora_ref[0, :, 0:128]
    xa = lora_ref[0, :, 128:256]
    dl = _dot(jnp.tanh(xw), wd_ref[...]) + w0_ref[...]
    lw = -DECAY_SCALE * _sigmoid(dl)
    lw_ref[0, 0] = lw[:, 0:512]
    lw_ref[1, 0] = lw[:, 512:1024]
    ic = _sigmoid(_dot(xa, wi_ref[...]) + a0_ref[...])
    ic_ref[0, 0] = ic[:, 0:512]
    ic_ref[1, 0] = ic[:, 512:1024]


def _rwkv_prep(rkv, lora, conv_w, wd, w0, wi, a0, k_k, gsum):
    B, T, _ = rkv.shape
    bm = ROW_BLOCK
    hb = bm // 8
    nh = T // 8
    row3 = lambda b, i: (b, i, 0)
    const2 = lambda b, i: (0, 0)
    o512 = jax.ShapeDtypeStruct((B, T, 512), f32)
    o2 = jax.ShapeDtypeStruct((2, B, T, 512), f32)
    return pl.pallas_call(
        _prep_kernel,
        grid=(B, T // bm),
        in_specs=[pl.BlockSpec((1, bm, 1536), row3),
                  pl.BlockSpec((1, 8, 1536), lambda b, i: (b, jnp.maximum(i * hb - 1, 0), 0)),
                  pl.BlockSpec((1, 8, 1536), lambda b, i: (b, jnp.minimum((i + 1) * hb, nh - 1), 0)),
                  pl.BlockSpec((1, bm, 384), row3),
                  pl.BlockSpec((3, 1536), const2),
                  pl.BlockSpec((128, 1024), const2),
                  pl.BlockSpec((1, 1024), const2),
                  pl.BlockSpec((128, 1024), const2),
                  pl.BlockSpec((1, 1024), const2),
                  pl.BlockSpec((1, 512), const2),
                  pl.BlockSpec((512, 512), const2)],
        out_specs=[pl.BlockSpec((1, bm, 512), row3)] * 4
        + [pl.BlockSpec((2, 1, bm, 512), lambda b, i: (0, b, i, 0))] * 2,
        out_shape=[o512, o512, o512, o512, o2, o2],
        compiler_params=_cparams(("parallel", "parallel")),
        name="rwkv_prep",
    )(rkv, rkv, rkv, lora, conv_w, wd, w0, wi, a0, k_k, gsum)


def _scan_kernel(r_ref, k_ref, v_ref, kk_ref, lw_ref, ic_ref, ka_ref, y_ref, h_ref, *, nc):
    L = CHUNK
    d = pl.program_id(0)
    fwd = d == 0

    @pl.when(pl.program_id(2) == 0)
    def _():
        h_ref[...] = jnp.zeros_like(h_ref)

    sgn = jnp.where(fwd, 1, -1)
    col = _iota((L, LANES), 1)
    row = _iota((L, LANES), 0)
    order = ((col & (L - 1)) - row) * sgn
    left = col < L
    strict_l = (order < 0) & left
    strict_r = (order < 0) & jnp.logical_not(left)
    incl = order <= 0
    eye_r = jnp.where(col - L == row, 1.0, 0.0)
    right_f = jnp.where(left, 0.0, 1.0)
    incl_bf = jnp.where(order[:, 0:L] <= 0, 1.0, 0.0).astype(bf16)
    lane = _iota((1, LANES), 1)
    head_m = (jnp.where(lane < L, 1.0, 0.0), jnp.where(lane < L, 0.0, 1.0))
    blockdiag = (_iota((LANES, LANES), 0) < L) == (_iota((LANES, LANES), 1) < L)
    eye128 = _iota((LANES, LANES), 0) == _iota((LANES, LANES), 1)
    z128 = jnp.zeros((L, LANES), f32)
    z256 = jnp.zeros((L, 2 * LANES), f32)
    ka = ka_ref[...]
    pairs = [(i, p) for i in range(nc) for p in range(4)]
    units = [(i, p, hh) for (i, p) in pairs for hh in range(2)]

    rows_of, ch = [], []
    for i in range(nc):
        ci = jnp.where(fwd, i, nc - 1 - i)
        rows = pl.ds(pl.multiple_of(ci * L, L), L)
        rows_of.append(rows)
        lw = lw_ref[0, 0, rows, :]
        ic = ic_ref[0, 0, rows, :]
        kk = kk_ref[0, rows, :]
        kd = k_ref[0, rows, :] * (1.0 + (ic - 1.0) * ka)
        b = kk * ic
        l1 = lw.astype(bf16)
        e1 = lw - l1.astype(f32)
        l2 = e1.astype(bf16)
        l3 = (e1 - l2.astype(f32)).astype(bf16)
        cum = (jnp.dot(incl_bf, l1, preferred_element_type=f32)
               + jnp.dot(incl_bf, l2, preferred_element_type=f32)
               + jnp.dot(incl_bf, l3, preferred_element_type=f32))
        tot = jnp.where(fwd, cum[L - 1:L, :], cum[0:1, :])
        p_inv = jnp.exp(-cum)
        p_end = jnp.exp(tot - cum)
        ch.append(dict(
            v=v_ref[0, rows, :], p_tot=jnp.exp(tot),
            rt=r_ref[0, rows, :] * jnp.exp(cum), at=-kk * jnp.exp(cum - lw),
            bt=b * p_inv, kt=kd * p_inv, bh=b * p_end, kh=kd * p_end))

    def sl(i, p, name):
        return ch[i][name][:, p * LANES:(p + 1) * LANES]

    am, vm, g = {}, {}, {}
    for (i, p) in pairs:
        z = jnp.concatenate([sl(i, p, "bt"), sl(i, p, "kt")], axis=0).astype(bf16)
        for hh in range(2):
            u = (i, p, hh)
            am[u] = sl(i, p, "at") * head_m[hh]
            vm[u] = sl(i, p, "v") * head_m[hh]
            xm = jnp.concatenate([am[u], sl(i, p, "rt") * head_m[hh]], axis=0)
            g[u] = _dot_nt(xm, z)
    pk, w1 = {}, {}
    for u in units:
        gt = g[u][0:L]
        pk[u] = jnp.where(strict_l, gt, 0.0) + eye_r
        w1[u] = _dot(jnp.where(strict_r, gt, 0.0), jnp.concatenate([z128, vm[u]], axis=0))
    for _ in range(6):
        for u in units:
            prod = _dot(pk[u], jnp.concatenate([pk[u], z128], axis=0))
            pk[u] = prod + pk[u] * right_f
    tu, ry = {}, {}
    for u in units:
        tu[u] = _dot(pk[u], jnp.concatenate([z256, jnp.concatenate([am[u], w1[u]], axis=1)], axis=0))
    for u in units:
        lhs = jnp.where(incl, g[u][L:2 * L], 0.0)
        ry[u] = _dot(lhs, jnp.concatenate([tu[u], jnp.concatenate([z128, vm[u]], axis=1)], axis=0))
    r_pair, y_pair, m_p, c_p, p_col = {}, {}, {}, {}, {}
    for (i, p) in pairs:
        u0, u1 = (i, p, 0), (i, p, 1)
        t2 = tu[u0] + tu[u1]
        r2 = ry[u0] + ry[u1]
        r_pair[i, p] = sl(i, p, "rt") + r2[:, 0:LANES]
        y_pair[i, p] = r2[:, LANES:]
        zl = jnp.concatenate([sl(i, p, "bh"), sl(i, p, "kh")], axis=0)
        zr = jnp.concatenate([t2, jnp.concatenate([z128, sl(i, p, "v")], axis=1)], axis=0)
        mc = _dot_tn(zl, zr)
        m_p[i, p] = jnp.where(blockdiag, mc[:, 0:LANES], 0.0)
        c_p[i, p] = jnp.where(blockdiag, mc[:, LANES:], 0.0)
        p_col[i, p] = jnp.sum(jnp.where(eye128, sl(i, p, "p_tot"), 0.0), axis=1, keepdims=True)
    hs = [h_ref[p] for p in range(4)]
    for i in range(nc):
        for p in range(4):
            sd = _dot(jnp.concatenate([r_pair[i, p], m_p[i, p]], axis=0), hs[p])
            y_ref[0, 0, rows_of[i], p * LANES:(p + 1) * LANES] = sd[0:L] + y_pair[i, p]
            hs[p] = p_col[i, p] * hs[p] + sd[L:] + c_p[i, p]
    for p in range(4):
        h_ref[p] = hs[p]


def _rwkv_scan(r, k, v, kk, lw, ic, k_a):
    B, T, _ = r.shape
    nc = SCAN_CHUNKS
    lb = nc * CHUNK
    ns = T // lb

    def blk(d, b, s):
        return jnp.where(d == 0, s, ns - 1 - s)

    shared = pl.BlockSpec((1, lb, 512), lambda d, b, s: (b, blk(d, b, s), 0))
    perdir = pl.BlockSpec((1, 1, lb, 512), lambda d, b, s: (d, b, blk(d, b, s), 0))
    return pl.pallas_call(
        functools.partial(_scan_kernel, nc=nc),
        grid=(2, B, ns),
        in_specs=[shared, shared, shared, shared, perdir, perdir,
                  pl.BlockSpec((1, 512), lambda d, b, s: (0, 0))],
        out_specs=perdir,
        out_shape=jax.ShapeDtypeStruct((2, B, T, 512), f32),
        scratch_shapes=[pltpu.VMEM((4, LANES, LANES), f32)],
        compiler_params=_cparams(("arbitrary", "arbitrary", "arbitrary")),
        name="rwkv_scan",
    )(r, k, v, kk, lw, ic, k_a)


def _attn_kernel(q_ref, k_ref, vt_ref, o_ref, m_ref, acc_ref):
    ki = pl.program_id(3)
    bk = k_ref.shape[1]

    @pl.when(ki == 0)
    def _():
        m_ref[...] = jnp.full_like(m_ref, -1e30)
        acc_ref[...] = jnp.zeros_like(acc_ref)

    heads = [(j, hh) for j in range(2) for hh in range(2)]
    m_old = [m_ref[2 * j + hh] for (j, hh) in heads]
    s = [lax.dot_general(k_ref[0, :, hh * LANES:(hh + 1) * LANES], q_ref[0, :, j * LANES:(j + 1) * LANES],
                         (((1,), (1,)), ((), ())), preferred_element_type=f32) for (j, hh) in heads]
    first = ki == 0
    c = [jnp.where(first, 0.0, m_old[i]) for i in range(4)]
    d = [s[i] - c[i] for i in range(4)]
    t = [jnp.maximum(jnp.where(first, -1e30, 0.0), jnp.max(d[i], axis=0, keepdims=True)).astype(bf16)
         for i in range(4)]
    m_new = [c[i] + t[i].astype(f32) for i in range(4)]
    p = [jnp.concatenate([jnp.exp2(d[i][r0:r0 + ATTN_STRIP].astype(bf16) - t[i])
                          for r0 in range(0, bk, ATTN_STRIP)], axis=0) for i in range(4)]
    pv = [jnp.dot(vt_ref[0, hh * LANES:(hh + 1) * LANES, :], p[2 * j + hh], preferred_element_type=f32)
          for (j, hh) in heads]
    for i in range(4):
        acc_ref[i] = acc_ref[i] * jnp.exp2(m_old[i] - m_new[i]) + pv[i]
        m_ref[i] = m_new[i]

    @pl.when(ki == pl.num_programs(3) - 1)
    def _():
        row0 = _iota((LANES, 1), 0) < HEAD_DIM
        for j in range(2):
            a0 = acc_ref[2 * j]
            a1 = acc_ref[2 * j + 1]
            o = jnp.where(row0, a0 * (1.0 / a0[HEAD_DIM:HEAD_DIM + 1, :]), a1 * (1.0 / a1[0:1, :]))
            o_ref[0, :, j * LANES:(j + 1) * LANES] = o.T.astype(o_ref.dtype)


def _attention(q, k, vt):
    B, T, _ = q.shape
    bq, bk = ATTN_BQ, ATTN_BK
    return pl.pallas_call(
        _attn_kernel,
        grid=(B, KV_HEADS, T // bq, T // bk),
        in_specs=[pl.BlockSpec((1, bq, 256), lambda b, g, qi, ki: (b, qi, g)),
                  pl.BlockSpec((1, bk, 256), lambda b, g, qi, ki: (b, ki, g)),
                  pl.BlockSpec((1, 256, bk), lambda b, g, qi, ki: (b, g, ki))],
        out_specs=pl.BlockSpec((1, bq, 256), lambda b, g, qi, ki: (b, qi, g)),
        out_shape=jax.ShapeDtypeStruct((B, T, ATTN_DIM), bf16),
        scratch_shapes=[pltpu.VMEM((4, 1, bq), f32), pltpu.VMEM((4, LANES, bq), f32)],
        compiler_params=_cparams(("parallel", "parallel", "parallel", "arbitrary")),
        name="gqa_attention",
    )(q, k, vt)


def _mix_kernel(x_ref, y_ref, r_ref, k_ref, v_ref, ic_ref, lora_ref, attn_ref, mod_ref, gs_ref,
                gup_ref, wo_ref, lng_ref, lnb_ref, rk_ref, ka_ref, o_ref):
    gs = gs_ref[...]
    y = y_ref[0, 0] + y_ref[1, 0]
    mu = _dot_exact_rhs(y, gs) * (1.0 / HEAD_DIM)
    yc = y - mu
    var = _dot_exact_rhs(yc * yc, gs) * (1.0 / HEAD_DIM)
    yn = yc * lax.rsqrt(var + GN_EPS) * lng_ref[...] + lnb_ref[...]
    icm = 0.5 * (ic_ref[0, 0] + ic_ref[1, 0])
    r = r_ref[0]
    kb = k_ref[0] * (1.0 + (icm - 1.0) * ka_ref[...])
    bonus = _dot_exact_rhs(r * kb * rk_ref[...], gs) * v_ref[0]
    g = _dot(_sigmoid(lora_ref[0, :, 256:384]), gup_ref[...])
    rw = ((yn + bonus) * g).astype(bf16)
    mix = (jnp.dot(rw, wo_ref[0:RWKV_DIM, :], preferred_element_type=f32)
           + jnp.dot(attn_ref[0], wo_ref[RWKV_DIM:, :], preferred_element_type=f32))
    o_ref[0] = x_ref[0] + mod_ref[0, 2:3, :] * mix


def _mix(x, y, r, k, v, ic, lora, attn, mod, gsum, gate_up, w_out, ln_g, ln_b, r_k, k_a):
    B, T, _ = x.shape
    bm = ROW_BLOCK
    row3 = lambda b, i: (b, i, 0)
    dir4 = lambda b, i: (0, b, i, 0)
    const2 = lambda b, i: (0, 0)
    vec512 = pl.BlockSpec((1, 512), const2)
    return pl.pallas_call(
        _mix_kernel,
        grid=(B, T // bm),
        in_specs=[pl.BlockSpec((1, bm, D_MODEL), row3),
                  pl.BlockSpec((2, 1, bm, 512), dir4),
                  pl.BlockSpec((1, bm, 512), row3),
                  pl.BlockSpec((1, bm, 512), row3),
                  pl.BlockSpec((1, bm, 512), row3),
                  pl.BlockSpec((2, 1, bm, 512), dir4),
                  pl.BlockSpec((1, bm, 384), row3),
                  pl.BlockSpec((1, bm, 512), row3),
                  pl.BlockSpec((1, 6, D_MODEL), lambda b, i: (b, 0, 0)),
                  pl.BlockSpec((512, 512), const2),
                  pl.BlockSpec((128, 512), const2),
                  pl.BlockSpec((D_MODEL, D_MODEL), const2),
                  vec512, vec512, vec512, vec512],
        out_specs=pl.BlockSpec((1, bm, D_MODEL), row3),
        out_shape=jax.ShapeDtypeStruct((B, T, D_MODEL), f32),
        compiler_params=_cparams(("parallel", "parallel")),
        name="mix_out",
    )(x, y, r, k, v, ic, lora, attn, mod, gsum, gate_up, w_out, ln_g, ln_b, r_k, k_a)


def _ffn_kernel(x_ref, mod_ref, ng_ref, wg_ref, wu_ref, wo_ref, o_ref, h_ref, acc_ref):
    c = pl.program_id(2)

    @pl.when(c == 0)
    def _():
        x = x_ref[0]
        ms = jnp.mean(x * x, axis=-1, keepdims=True)
        h = x * lax.rsqrt(ms + NORM_EPS) * ng_ref[...]
        h_ref[...] = (h * (1.0 + mod_ref[0, 4:5, :]) + mod_ref[0, 3:4, :]).astype(bf16)
        acc_ref[...] = jnp.zeros_like(acc_ref)

    h = h_ref[...]
    gt = jnp.dot(h, wg_ref[...], preferred_element_type=f32)
    up = jnp.dot(h, wu_ref[...], preferred_element_type=f32)
    act = (gt * _sigmoid(gt) * up).astype(bf16)
    acc_ref[...] += jnp.dot(act, wo_ref[...], preferred_element_type=f32)

    @pl.when(c == pl.num_programs(2) - 1)
    def _():
        o_ref[0] = x_ref[0] + mod_ref[0, 5:6, :] * acc_ref[...]


def _ffn(x, mod, norm_g, w_in, w_out):
    B, T, _ = x.shape
    bm = ROW_BLOCK
    ncf = D_FF // FFN_CHUNK
    row3 = lambda b, i, c: (b, i, 0)
    return pl.pallas_call(
        _ffn_kernel,
        grid=(B, T // bm, ncf),
        in_specs=[pl.BlockSpec((1, bm, D_MODEL), row3),
                  pl.BlockSpec((1, 6, D_MODEL), lambda b, i, c: (b, 0, 0)),
                  pl.BlockSpec((1, D_MODEL), lambda b, i, c: (0, 0)),
                  pl.BlockSpec((D_MODEL, FFN_CHUNK), lambda b, i, c: (0, c)),
                  pl.BlockSpec((D_MODEL, FFN_CHUNK), lambda b, i, c: (0, ncf + c)),
                  pl.BlockSpec((FFN_CHUNK, D_MODEL), lambda b, i, c: (c, 0))],
        out_specs=pl.BlockSpec((1, bm, D_MODEL), row3),
        out_shape=jax.ShapeDtypeStruct((B, T, D_MODEL), f32),
        scratch_shapes=[pltpu.VMEM((bm, D_MODEL), bf16), pltpu.VMEM((bm, D_MODEL), f32)],
        compiler_params=_cparams(("parallel", "parallel", "arbitrary")),
        name="ffn",
    )(x, mod, norm_g, w_in, w_in, w_out)


def _rope_tables(T):
    pos = jnp.arange(T, dtype=jnp.int32)
    row = (pos // GRID_W).astype(f32)
    col = (pos % GRID_W).astype(f32)
    quarter = HEAD_DIM // 4
    freq = 1.0 / (ROPE_THETA ** (jnp.arange(quarter, dtype=f32) / quarter))
    ang_r = row[:, None] * freq[None, :]
    ang_c = col[:, None] * freq[None, :]
    cos = jnp.concatenate([jnp.cos(ang_r)] * 2 + [jnp.cos(ang_c)] * 2, axis=1)
    sin = jnp.concatenate([-jnp.sin(ang_r), jnp.sin(ang_r), -jnp.sin(ang_c), jnp.sin(ang_c)], axis=1)
    return jnp.concatenate([cos, cos], axis=1), jnp.concatenate([sin, sin], axis=1)


def _block_lora(up):
    z = jnp.zeros_like(up[0])
    return jnp.concatenate([jnp.concatenate([up[0], z], axis=1),
                            jnp.concatenate([z, up[1]], axis=1)], axis=0)


def _layer_params(l, ada_w, ada_b, norm_mix_g, norm_ffn_g, w_in, conv_w, decay_w0, decay_up, iclr_a0,
                  iclr_up, gate_up, k_k, k_a, r_k, ln_x_g, ln_x_b, q_norm_g, k_norm_g, w_out,
                  w_ffn_in, w_ffn_out):
    wi = w_in[l]
    wk = wi[:, 2048:2176]
    wv = wi[:, 2176:2304]
    z64 = jnp.zeros((D_MODEL, HEAD_DIM), wi.dtype)
    dup = lambda w: jnp.concatenate([w[:, 0:64], z64, z64, w[:, 0:64], w[:, 64:128], z64, z64, w[:, 64:128]],
                                    axis=1)
    w_all = jnp.concatenate([wi[:, 0:1536], wi[:, 2304:2688], wi[:, 1536:2048], dup(wk), dup(wv)],
                            axis=1).astype(bf16)
    return dict(
        ada_w=ada_w[l], ada_b=ada_b[l],
        norm_mix_g=norm_mix_g[l].reshape(1, -1), norm_ffn_g=norm_ffn_g[l].reshape(1, -1),
        w_all=w_all, conv_w=conv_w[l],
        wd=_block_lora(decay_up[l]).astype(bf16), w0=decay_w0[l].reshape(1, -1),
        wi=_block_lora(iclr_up[l]).astype(bf16), a0=iclr_a0[l].reshape(1, -1),
        gate_up=gate_up[l].astype(bf16),
        k_k=k_k[l].reshape(1, -1), k_a=k_a[l].reshape(1, -1), r_k=r_k[l].reshape(1, -1),
        ln_g=ln_x_g[l].reshape(1, -1), ln_b=ln_x_b[l].reshape(1, -1),
        qg=jnp.tile(q_norm_g[l], 8).reshape(1, -1), kg=jnp.tile(k_norm_g[l], 8).reshape(1, -1),
        w_out=w_out[l].astype(bf16), w_ffn_in=w_ffn_in[l].astype(bf16),
        w_ffn_out=w_ffn_out[l].astype(bf16))


def _layer(x, c, p, cos, sin, gsum):
    mod = _modulation(c, p["ada_w"], p["ada_b"])
    rkv, lora, q, k, vt = _in_proj(x, mod, p["norm_mix_g"], p["w_all"], cos, sin, p["qg"], p["kg"], gsum)
    r, kr, vr, kk, lw, ic = _rwkv_prep(rkv, lora, p["conv_w"], p["wd"], p["w0"], p["wi"], p["a0"],
                                       p["k_k"], gsum)
    y = _rwkv_scan(r, kr, vr, kk, lw, ic, p["k_a"])
    attn = _attention(q, k, vt)
    x1 = _mix(x, y, r, kr, vr, ic, lora, attn, mod, gsum, p["gate_up"], p["w_out"], p["ln_g"],
              p["ln_b"], p["r_k"], p["k_a"])
    return _ffn(x1, mod, p["norm_ffn_g"], p["w_ffn_in"], p["w_ffn_out"])


def kernel(x_prompt, x_sample, c_prompt, c_sample, ada_w, ada_b, norm_mix_g, norm_ffn_g, w_in, conv_w,
           decay_w0, decay_up, iclr_a0, iclr_up, gate_up, k_k, k_a, r_k, ln_x_g, ln_x_b, q_norm_g,
           k_norm_g, w_out, w_ffn_in, w_ffn_out):
    depth = ada_w.shape[0]
    params = [_layer_params(l, ada_w, ada_b, norm_mix_g, norm_ffn_g, w_in, conv_w, decay_w0, decay_up,
                            iclr_a0, iclr_up, gate_up, k_k, k_a, r_k, ln_x_g, ln_x_b, q_norm_g,
                            k_norm_g, w_out, w_ffn_in, w_ffn_out) for l in range(depth)]
    head = np.arange(512) // HEAD_DIM
    gsum = jnp.asarray(head[:, None] == head[None, :], dtype=bf16)

    def run_trunk(x, c):
        cos, sin = _rope_tables(x.shape[1])
        for p in params:
            x = _layer(x, c, p, cos, sin, gsum)
        return x

    return (run_trunk(x_prompt, c_prompt), run_trunk(x_sample, c_sample))
```

```python
import functools
import math

import jax
import jax.numpy as jnp
import numpy as np
from jax import lax
from jax.experimental import pallas as pl
from jax.experimental.pallas import tpu as pltpu

f32 = jnp.float32
bf16 = jnp.bfloat16

D_MODEL = 1024
HEAD_DIM = 64
RWKV_DIM = 512
ATTN_DIM = 512
KV_HEADS = 2
LORA = 64
D_FF = 2816
GRID_W = 64
ROPE_THETA = 10000.0
NORM_EPS = 1e-6
QK_EPS = 1e-6
GN_EPS = 64e-5
DECAY_SCALE = math.exp(-0.5)

LANES = 128
VMEM_LIMIT = 48 * 1024 * 1024

ROW_BLOCK = 512
FFN_ROWS = 512
CHUNK = 64
SCAN_CHUNKS = 4
ATTN_BQ = 1024
ATTN_BK = 1024
ATTN_QSUB = 512
ATTN_STRIP = 16
LOG2E = 1.4426950408889634

C_RKV = 0
C_LORA = 1536
C_Q = 1920
C_K = 2432
C_V = 2944
C_END = 3456


def _cparams(sem):
    return pltpu.CompilerParams(dimension_semantics=sem, vmem_limit_bytes=VMEM_LIMIT)


def _dot(a, b):
    return jnp.dot(a.astype(bf16), b.astype(bf16), preferred_element_type=f32)


def _dot_nt(a, b):
    return lax.dot_general(a.astype(bf16), b.astype(bf16), (((1,), (1,)), ((), ())),
                           preferred_element_type=f32)


def _dot_tn(a, b):
    return lax.dot_general(a.astype(bf16), b.astype(bf16), (((0,), (0,)), ((), ())),
                           preferred_element_type=f32)


def _split2(a):
    hi = a.astype(bf16)
    lo = (a - hi.astype(f32)).astype(bf16)
    return hi, lo


def _dot_exact_rhs(a, g):
    hi, lo = _split2(a)
    return (jnp.dot(hi, g, preferred_element_type=f32) + jnp.dot(lo, g, preferred_element_type=f32))


def _sigmoid(x):
    return 1.0 / (1.0 + jnp.exp(-x))


def _iota(shape, dim):
    return lax.broadcasted_iota(jnp.int32, shape, dim)


def _mod_kernel(c_ref, w_ref, b_ref, o_ref):
    c = c_ref[...]
    s = c * _sigmoid(c)
    sh, sl = _split2(s)
    wh, wl = _split2(w_ref[...])
    acc = jnp.dot(sh, wh, preferred_element_type=f32)
    acc += jnp.dot(sh, wl, preferred_element_type=f32)
    acc += jnp.dot(sl, wh, preferred_element_type=f32)
    o_ref[...] = acc + b_ref[...]


def _modulation(c, ada_w, ada_b):
    B = c.shape[0]
    Bp = max(8, B)
    cp = jnp.pad(c, ((0, Bp - B), (0, 0)))
    n = ada_w.shape[1] // D_MODEL
    out = pl.pallas_call(
        _mod_kernel,
        grid=(n,),
        in_specs=[pl.BlockSpec((Bp, D_MODEL), lambda j: (0, 0)),
                  pl.BlockSpec((D_MODEL, D_MODEL), lambda j: (0, j)),
                  pl.BlockSpec((1, D_MODEL), lambda j: (0, j))],
        out_specs=pl.BlockSpec((Bp, D_MODEL), lambda j: (0, j)),
        out_shape=jax.ShapeDtypeStruct((Bp, n * D_MODEL), f32),
        compiler_params=_cparams(("arbitrary",)),
        name="adaln_mod",
    )(cp, ada_w, ada_b.reshape(1, -1))
    return out[:B].reshape(B, n, D_MODEL)


def _rope(x, cos, sin):
    w = x.shape[1]
    up = pltpu.roll(x, w - 16, 1)
    dn = pltpu.roll(x, 16, 1)
    first = (_iota((1, w), 1) & 16) == 0
    return x * cos + jnp.where(first, up, dn) * sin


def _qk_norm(x, g_ref, gain):
    w = x.shape[1]
    ms = _dot_exact_rhs(x * x, g_ref[0:w, 0:w]) * (1.0 / HEAD_DIM)
    return x * lax.rsqrt(ms + QK_EPS) * gain


def _in_kernel(x_ref, mod_ref, ng_ref, w_ref, cos_ref, sin_ref, qg_ref, kg_ref, gs_ref,
               rkv_ref, lora_ref, q_ref, k_ref, v_ref, vs_ref):
    x = x_ref[0]
    ms = jnp.mean(x * x, axis=-1, keepdims=True)
    h = x * lax.rsqrt(ms + NORM_EPS) * ng_ref[...]
    h = h * (1.0 + mod_ref[0, 1:2, :]) + mod_ref[0, 0:1, :]
    hb = h.astype(bf16)
    rkv_ref[0] = jnp.dot(hb, w_ref[:, C_RKV:C_LORA], preferred_element_type=f32)
    lora_ref[0] = jnp.dot(hb, w_ref[:, C_LORA:C_Q], preferred_element_type=f32)
    q = jnp.dot(hb, w_ref[:, C_Q:C_K], preferred_element_type=f32)
    k = jnp.dot(hb, w_ref[:, C_K:C_V], preferred_element_type=f32)
    vs_ref[...] = jnp.dot(hb, w_ref[:, C_V:C_END], preferred_element_type=f32)
    vrow = _iota((ATTN_DIM, 1), 0)
    ones_row = (vrow == 64) | (vrow == 128) | (vrow == 320) | (vrow == 384)
    v_ref[0] = jnp.where(ones_row, 1.0, vs_ref[...].T).astype(bf16)
    cos = cos_ref[...]
    sin = sin_ref[...]
    cos4 = jnp.concatenate([cos] * 4, axis=1)
    sin4 = jnp.concatenate([sin] * 4, axis=1)
    qn = _rope(_qk_norm(q, gs_ref, qg_ref[...]), cos4, sin4)
    q_ref[0] = (qn * (LOG2E * HEAD_DIM ** -0.5)).astype(bf16)
    kn = _rope(_qk_norm(k, gs_ref, kg_ref[...]), cos4, sin4)
    k_ref[0] = kn.astype(bf16)


def _in_proj(x, mod, norm_g, w_all, cos, sin, qg, kg, gsum):
    B, T, _ = x.shape
    bm = ROW_BLOCK
    row3 = lambda b, i: (b, i, 0)
    const2 = lambda b, i: (0, 0)
    return pl.pallas_call(
        _in_kernel,
        grid=(B, T // bm),
        in_specs=[pl.BlockSpec((1, bm, D_MODEL), row3),
                  pl.BlockSpec((1, 6, D_MODEL), lambda b, i: (b, 0, 0)),
                  pl.BlockSpec((1, D_MODEL), const2),
                  pl.BlockSpec((D_MODEL, C_END), const2),
                  pl.BlockSpec((bm, LANES), lambda b, i: (i, 0)),
                  pl.BlockSpec((bm, LANES), lambda b, i: (i, 0)),
                  pl.BlockSpec((1, 512), const2),
                  pl.BlockSpec((1, 512), const2),
                  pl.BlockSpec((512, 512), const2)],
        out_specs=[pl.BlockSpec((1, bm, 1536), row3),
                   pl.BlockSpec((1, bm, 384), row3),
                   pl.BlockSpec((1, bm, 512), row3),
                   pl.BlockSpec((1, bm, 512), row3),
                   pl.BlockSpec((1, 512, bm), lambda b, i: (b, 0, i))],
        out_shape=[jax.ShapeDtypeStruct((B, T, 1536), f32),
                   jax.ShapeDtypeStruct((B, T, 384), f32),
                   jax.ShapeDtypeStruct((B, T, 512), bf16),
                   jax.ShapeDtypeStruct((B, T, 512), bf16),
                   jax.ShapeDtypeStruct((B, 512, T), bf16)],
        scratch_shapes=[pltpu.VMEM((bm, 512), f32)],
        compiler_params=_cparams(("parallel", "parallel")),
        name="in_proj",
    )(x, mod, norm_g, w_all, cos, sin, qg, kg, gsum)


def _prep_kernel(cur_ref, prev_ref, next_ref, lora_ref, cw_ref, wd_ref, w0_ref, wi_ref, a0_ref,
                 kk_ref, ka_ref, rk_ref, gs_ref, r_ref, k_ref, v_ref, kkn_ref, lw_ref, ic_ref, bonus_ref):
    i = pl.program_id(1)
    n = pl.num_programs(1)
    bm = cur_ref.shape[1]
    rows = _iota((bm, 1), 0)
    has_prev = jnp.where(i > 0, 1.0, 0.0)
    has_next = jnp.where(i < n - 1, 1.0, 0.0)
    outs = (r_ref, k_ref, v_ref)
    rkv = []
    for p in range(3):
        cols = slice(p * 512, (p + 1) * 512)
        cur = cur_ref[0, :, cols]
        prev_row = prev_ref[0, 7:8, cols] * has_prev
        next_row = next_ref[0, 0:1, cols] * has_next
        xm1 = jnp.where(rows == 0, prev_row, pltpu.roll(cur, 1, 0))
        xp1 = jnp.where(rows == bm - 1, next_row, pltpu.roll(cur, bm - 1, 0))
        y = cw_ref[0:1, cols] * xm1 + cw_ref[1:2, cols] * cur + cw_ref[2:3, cols] * xp1
        outs[p][0] = y
        rkv.append(y)
        if p == 1:
            kkh = y * kk_ref[...]
            ss = _dot_exact_rhs(kkh * kkh, gs_ref[...])
            kkn_ref[0] = kkh * lax.rsqrt(ss + 1e-12)
    xw = lora_ref[0, :, 0:128]
    xa = lora_ref[0, :, 128:256]
    dl = _dot(jnp.tanh(xw), wd_ref[...]) + w0_ref[...]
    lw = -DECAY_SCALE * _sigmoid(dl)
    lw_ref[0, 0] = lw[:, 0:512]
    lw_ref[1, 0] = lw[:, 512:1024]
    ic = _sigmoid(_dot(xa, wi_ref[...]) + a0_ref[...])
    ic_ref[0, 0] = ic[:, 0:512]
    ic_ref[1, 0] = ic[:, 512:1024]
    r, k, v = rkv
    kb = k * (1.0 + (0.5 * (ic[:, 0:512] + ic[:, 512:1024]) - 1.0) * ka_ref[...])
    bonus_ref[0] = _dot_exact_rhs(r * kb * rk_ref[...], gs_ref[...]) * v


def _rwkv_prep(rkv, lora, conv_w, wd, w0, wi, a0, k_k, k_a, r_k, gsum):
    B, T, _ = rkv.shape
    bm = ROW_BLOCK
    hb = bm // 8
    nh = T // 8
    row3 = lambda b, i: (b, i, 0)
    const2 = lambda b, i: (0, 0)
    o512 = jax.ShapeDtypeStruct((B, T, 512), f32)
    o2 = jax.ShapeDtypeStruct((2, B, T, 512), f32)
    return pl.pallas_call(
        _prep_kernel,
        grid=(B, T // bm),
        in_specs=[pl.BlockSpec((1, bm, 1536), row3),
                  pl.BlockSpec((1, 8, 1536), lambda b, i: (b, jnp.maximum(i * hb - 1, 0), 0)),
                  pl.BlockSpec((1, 8, 1536), lambda b, i: (b, jnp.minimum((i + 1) * hb, nh - 1), 0)),
                  pl.BlockSpec((1, bm, 384), row3),
                  pl.BlockSpec((3, 1536), const2),
                  pl.BlockSpec((128, 1024), const2),
                  pl.BlockSpec((1, 1024), const2),
                  pl.BlockSpec((128, 1024), const2),
                  pl.BlockSpec((1, 1024), const2),
                  pl.BlockSpec((1, 512), const2),
                  pl.BlockSpec((1, 512), const2),
                  pl.BlockSpec((1, 512), const2),
                  pl.BlockSpec((512, 512), const2)],
        out_specs=[pl.BlockSpec((1, bm, 512), row3)] * 4
        + [pl.BlockSpec((2, 1, bm, 512), lambda b, i: (0, b, i, 0))] * 2
        + [pl.BlockSpec((1, bm, 512), row3)],
        out_shape=[o512, o512, o512, o512, o2, o2, o512],
        compiler_params=_cparams(("parallel", "parallel")),
        name="rwkv_prep",
    )(rkv, rkv, rkv, lora, conv_w, wd, w0, wi, a0, k_k, k_a, r_k, gsum)


def _scan_kernel(r_ref, k_ref, v_ref, kk_ref, lw_ref, ic_ref, ka_ref, y_ref, h_ref, *, nc):
    L = CHUNK
    d = pl.program_id(0)
    fwd = d == 0

    @pl.when(pl.program_id(2) == 0)
    def _():
        h_ref[...] = jnp.zeros_like(h_ref)

    sgn = jnp.where(fwd, 1, -1)
    col = _iota((L, LANES), 1)
    row = _iota((L, LANES), 0)
    order = ((col & (L - 1)) - row) * sgn
    left = col < L
    strict_l = (order < 0) & left
    strict_r = (order < 0) & jnp.logical_not(left)
    incl = order <= 0
    eye_r = jnp.where(col - L == row, 1.0, 0.0)
    right_f = jnp.where(left, 0.0, 1.0)
    incl_bf = jnp.where(order[:, 0:L] <= 0, 1.0, 0.0).astype(bf16)
    lane = _iota((1, LANES), 1)
    head_m = (jnp.where(lane < L, 1.0, 0.0), jnp.where(lane < L, 0.0, 1.0))
    blockdiag = (_iota((LANES, LANES), 0) < L) == (_iota((LANES, LANES), 1) < L)
    eye128 = _iota((LANES, LANES), 0) == _iota((LANES, LANES), 1)
    z128 = jnp.zeros((L, LANES), f32)
    z256 = jnp.zeros((L, 2 * LANES), f32)
    ka = ka_ref[...]
    pairs = [(i, p) for i in range(nc) for p in range(4)]
    units = [(i, p, hh) for (i, p) in pairs for hh in range(2)]

    rows_of, ch = [], []
    for i in range(nc):
        ci = jnp.where(fwd, i, nc - 1 - i)
        rows = pl.ds(pl.multiple_of(ci * L, L), L)
        rows_of.append(rows)
        lw = lw_ref[0, 0, rows, :]
        ic = ic_ref[0, 0, rows, :]
        kk = kk_ref[0, rows, :]
        kd = k_ref[0, rows, :] * (1.0 + (ic - 1.0) * ka)
        b = kk * ic
        l1 = lw.astype(bf16)
        e1 = lw - l1.astype(f32)
        l2 = e1.astype(bf16)
        l3 = (e1 - l2.astype(f32)).astype(bf16)
        cum = (jnp.dot(incl_bf, l1, preferred_element_type=f32)
               + jnp.dot(incl_bf, l2, preferred_element_type=f32)
               + jnp.dot(incl_bf, l3, preferred_element_type=f32))
        tot = jnp.where(fwd, cum[L - 1:L, :], cum[0:1, :])
        p_inv = jnp.exp(-cum)
        p_end = jnp.exp(tot - cum)
        ch.append(dict(
            v=v_ref[0, rows, :], p_tot=jnp.exp(tot),
            rt=r_ref[0, rows, :] * jnp.exp(cum), at=-kk * jnp.exp(cum - lw),
            bt=b * p_inv, kt=kd * p_inv, bh=b * p_end, kh=kd * p_end))

    def sl(i, p, name):
        return ch[i][name][:, p * LANES:(p + 1) * LANES]

    am, vm, g = {}, {}, {}
    for (i, p) in pairs:
        z = jnp.concatenate([sl(i, p, "bt"), sl(i, p, "kt")], axis=0).astype(bf16)
        for hh in range(2):
            u = (i, p, hh)
            am[u] = sl(i, p, "at") * head_m[hh]
            vm[u] = sl(i, p, "v") * head_m[hh]
            xm = jnp.concatenate([am[u], sl(i, p, "rt") * head_m[hh]], axis=0)
            g[u] = _dot_nt(xm, z)
    pk, w1 = {}, {}
    for u in units:
        gt = g[u][0:L]
        pk[u] = jnp.where(strict_l, gt, 0.0) + eye_r
        w1[u] = _dot(jnp.where(strict_r, gt, 0.0), jnp.concatenate([z128, vm[u]], axis=0))
    for _ in range(6):
        for u in units:
            prod = _dot(pk[u], jnp.concatenate([pk[u], z128], axis=0))
            pk[u] = prod + pk[u] * right_f
    tu, ry = {}, {}
    for u in units:
        tu[u] = _dot(pk[u], jnp.concatenate([z256, jnp.concatenate([am[u], w1[u]], axis=1)], axis=0))
    for u in units:
        lhs = jnp.where(incl, g[u][L:2 * L], 0.0)
        ry[u] = _dot(lhs, jnp.concatenate([tu[u], jnp.concatenate([z128, vm[u]], axis=1)], axis=0))
    r_pair, y_pair, m_p, c_p, p_col = {}, {}, {}, {}, {}
    for (i, p) in pairs:
        u0, u1 = (i, p, 0), (i, p, 1)
        t2 = tu[u0] + tu[u1]
        r2 = ry[u0] + ry[u1]
        r_pair[i, p] = sl(i, p, "rt") + r2[:, 0:LANES]
        y_pair[i, p] = r2[:, LANES:]
        zl = jnp.concatenate([sl(i, p, "bh"), sl(i, p, "kh")], axis=0)
        zr = jnp.concatenate([t2, jnp.concatenate([z128, sl(i, p, "v")], axis=1)], axis=0)
        mc = _dot_tn(zl, zr)
        m_p[i, p] = jnp.where(blockdiag, mc[:, 0:LANES], 0.0)
        c_p[i, p] = jnp.where(blockdiag, mc[:, LANES:], 0.0)
        p_col[i, p] = jnp.sum(jnp.where(eye128, sl(i, p, "p_tot"), 0.0), axis=1, keepdims=True)
    hs = [h_ref[p] for p in range(4)]
    for i in range(nc):
        for p in range(4):
            sd = _dot(jnp.concatenate([r_pair[i, p], m_p[i, p]], axis=0), hs[p])
            y_ref[0, 0, rows_of[i], p * LANES:(p + 1) * LANES] = sd[0:L] + y_pair[i, p]
            hs[p] = p_col[i, p] * hs[p] + sd[L:] + c_p[i, p]
    for p in range(4):
        h_ref[p] = hs[p]


def _rwkv_scan(r, k, v, kk, lw, ic, k_a):
    B, T, _ = r.shape
    nc = SCAN_CHUNKS
    lb = nc * CHUNK
    ns = T // lb

    def blk(d, b, s):
        return jnp.where(d == 0, s, ns - 1 - s)

    shared = pl.BlockSpec((1, lb, 512), lambda d, b, s: (b, blk(d, b, s), 0))
    perdir = pl.BlockSpec((1, 1, lb, 512), lambda d, b, s: (d, b, blk(d, b, s), 0))
    return pl.pallas_call(
        functools.partial(_scan_kernel, nc=nc),
        grid=(2, B, ns),
        in_specs=[shared, shared, shared, shared, perdir, perdir,
                  pl.BlockSpec((1, 512), lambda d, b, s: (0, 0))],
        out_specs=perdir,
        out_shape=jax.ShapeDtypeStruct((2, B, T, 512), f32),
        scratch_shapes=[pltpu.VMEM((4, LANES, LANES), f32)],
        compiler_params=_cparams(("arbitrary", "arbitrary", "arbitrary")),
        name="rwkv_scan",
    )(r, k, v, kk, lw, ic, k_a)


def _attn_kernel(q_ref, k_ref, vt_ref, o_ref, m_ref, acc_ref, *scr):
    ki = pl.program_id(3)
    bk = k_ref.shape[1]

    @pl.when(ki == 0)
    def _():
        m_ref[...] = jnp.full_like(m_ref, -1e30)
        acc_ref[...] = jnp.zeros_like(acc_ref)

    bq = q_ref.shape[1]
    units = [(j, hh, c0) for c0 in range(0, bq, ATTN_QSUB) for j in range(2) for hh in range(2)]
    n = len(units)
    s_refs, p_refs = scr[:n], scr[n:]
    first = ki == 0
    m_old, m_new = [None] * n, [None] * n

    def scores(u):
        j, hh, c0 = units[u]
        s_refs[u][...] = lax.dot_general(k_ref[0, :, hh * LANES:(hh + 1) * LANES],
                                         q_ref[0, c0:c0 + ATTN_QSUB, j * LANES:(j + 1) * LANES],
                                         (((1,), (1,)), ((), ())), preferred_element_type=f32)

    def softmax(u):
        j, hh, c0 = units[u]
        m_old[u] = m_ref[2 * j + hh, :, c0:c0 + ATTN_QSUB]
        c = jnp.where(first, 0.0, m_old[u])
        smax = jnp.max(s_refs[u][...], axis=0, keepdims=True)
        t = jnp.maximum(jnp.where(first, -1e30, 0.0), smax - c).astype(bf16)
        m_new[u] = c + t.astype(f32)
        for r0 in range(0, bk, ATTN_STRIP):
            rs = slice(r0, r0 + ATTN_STRIP)
            p_refs[u][rs, :] = jnp.exp2((s_refs[u][rs, :] - c).astype(bf16) - t)

    def update(u):
        j, hh, c0 = units[u]
        i = 2 * j + hh
        pv = jnp.dot(vt_ref[0, hh * LANES:(hh + 1) * LANES, :], p_refs[u][...],
                     preferred_element_type=f32)
        acc_ref[i, :, c0:c0 + ATTN_QSUB] = (acc_ref[i, :, c0:c0 + ATTN_QSUB]
                                            * jnp.exp2(m_old[u] - m_new[u]) + pv)
        m_ref[i, :, c0:c0 + ATTN_QSUB] = m_new[u]

    scores(0)
    for u in range(n):
        if u + 1 < n:
            scores(u + 1)
        softmax(u)
        if u > 0:
            update(u - 1)
    update(n - 1)

    @pl.when(ki == pl.num_programs(3) - 1)
    def _():
        row0 = _iota((LANES, 1), 0) < HEAD_DIM
        for j in range(2):
            a0 = acc_ref[2 * j]
            a1 = acc_ref[2 * j + 1]
            o = jnp.where(row0, a0 * (1.0 / a0[HEAD_DIM:HEAD_DIM + 1, :]), a1 * (1.0 / a1[0:1, :]))
            o_ref[0, :, j * LANES:(j + 1) * LANES] = o.T.astype(o_ref.dtype)


def _attention(q, k, vt):
    B, T, _ = q.shape
    bq, bk = ATTN_BQ, ATTN_BK
    nu = 4 * (bq // ATTN_QSUB)
    return pl.pallas_call(
        _attn_kernel,
        grid=(B, KV_HEADS, T // bq, T // bk),
        in_specs=[pl.BlockSpec((1, bq, 256), lambda b, g, qi, ki: (b, qi, g)),
                  pl.BlockSpec((1, bk, 256), lambda b, g, qi, ki: (b, ki, g)),
                  pl.BlockSpec((1, 256, bk), lambda b, g, qi, ki: (b, g, ki))],
        out_specs=pl.BlockSpec((1, bq, 256), lambda b, g, qi, ki: (b, qi, g)),
        out_shape=jax.ShapeDtypeStruct((B, T, ATTN_DIM), bf16),
        scratch_shapes=[pltpu.VMEM((4, 1, bq), f32), pltpu.VMEM((4, LANES, bq), f32)]
        + [pltpu.VMEM((bk, ATTN_QSUB), f32)] * nu + [pltpu.VMEM((bk, ATTN_QSUB), bf16)] * nu,
        compiler_params=_cparams(("parallel", "parallel", "parallel", "arbitrary")),
        name="gqa_attention",
    )(q, k, vt)


def _mix_kernel(x_ref, y_ref, bonus_ref, lora_ref, attn_ref, mod_ref, gs_ref, gup_ref, wo_ref, lng_ref,
                lnb_ref, o_ref):
    gs = gs_ref[...]
    y = y_ref[0, 0] + y_ref[1, 0]
    mu = _dot_exact_rhs(y, gs) * (1.0 / HEAD_DIM)
    yc = y - mu
    var = _dot_exact_rhs(yc * yc, gs) * (1.0 / HEAD_DIM)
    yn = yc * lax.rsqrt(var + GN_EPS) * lng_ref[...] + lnb_ref[...]
    g = _dot(_sigmoid(lora_ref[0, :, 256:384]), gup_ref[...])
    rw = ((yn + bonus_ref[0]) * g).astype(bf16)
    mix = (jnp.dot(rw, wo_ref[0:RWKV_DIM, :], preferred_element_type=f32)
           + jnp.dot(attn_ref[0], wo_ref[RWKV_DIM:, :], preferred_element_type=f32))
    o_ref[0] = x_ref[0] + mod_ref[0, 2:3, :] * mix


def _mix(x, y, bonus, lora, attn, mod, gsum, gate_up, w_out, ln_g, ln_b):
    B, T, _ = x.shape
    bm = ROW_BLOCK
    row3 = lambda b, i: (b, i, 0)
    dir4 = lambda b, i: (0, b, i, 0)
    const2 = lambda b, i: (0, 0)
    vec512 = pl.BlockSpec((1, 512), const2)
    return pl.pallas_call(
        _mix_kernel,
        grid=(B, T // bm),
        in_specs=[pl.BlockSpec((1, bm, D_MODEL), row3),
                  pl.BlockSpec((2, 1, bm, 512), dir4),
                  pl.BlockSpec((1, bm, 512), row3),
                  pl.BlockSpec((1, bm, 384), row3),
                  pl.BlockSpec((1, bm, 512), row3),
                  pl.BlockSpec((1, 6, D_MODEL), lambda b, i: (b, 0, 0)),
                  pl.BlockSpec((512, 512), const2),
                  pl.BlockSpec((128, 512), const2),
                  pl.BlockSpec((D_MODEL, D_MODEL), const2),
                  vec512, vec512],
        out_specs=pl.BlockSpec((1, bm, D_MODEL), row3),
        out_shape=jax.ShapeDtypeStruct((B, T, D_MODEL), f32),
        compiler_params=_cparams(("parallel", "parallel")),
        name="mix_out",
    )(x, y, bonus, lora, attn, mod, gsum, gate_up, w_out, ln_g, ln_b)


def _ffn_kernel(x_ref, mod_ref, ng_ref, wg_ref, wu_ref, wo_ref, o_ref):
    x = x_ref[0]
    ms = jnp.mean(x * x, axis=-1, keepdims=True)
    h = x * lax.rsqrt(ms + NORM_EPS) * ng_ref[...]
    h = (h * (1.0 + mod_ref[0, 4:5, :]) + mod_ref[0, 3:4, :]).astype(bf16)
    gt = jnp.dot(h, wg_ref[...], preferred_element_type=f32)
    up = jnp.dot(h, wu_ref[...], preferred_element_type=f32)
    act = (gt * _sigmoid(gt) * up).astype(bf16)
    o_ref[0] = x + mod_ref[0, 5:6, :] * jnp.dot(act, wo_ref[...], preferred_element_type=f32)


def _ffn(x, mod, norm_g, w_in, w_out):
    B, T, _ = x.shape
    bm = FFN_ROWS
    row3 = lambda b, i: (b, i, 0)
    resident = pl.Buffered(1)
    return pl.pallas_call(
        _ffn_kernel,
        grid=(B, T // bm),
        in_specs=[pl.BlockSpec((1, bm, D_MODEL), row3),
                  pl.BlockSpec((1, 6, D_MODEL), lambda b, i: (b, 0, 0)),
                  pl.BlockSpec((1, D_MODEL), lambda b, i: (0, 0)),
                  pl.BlockSpec((D_MODEL, D_FF), lambda b, i: (0, 0), pipeline_mode=resident),
                  pl.BlockSpec((D_MODEL, D_FF), lambda b, i: (0, 1), pipeline_mode=resident),
                  pl.BlockSpec((D_FF, D_MODEL), lambda b, i: (0, 0), pipeline_mode=resident)],
        out_specs=pl.BlockSpec((1, bm, D_MODEL), row3),
        out_shape=jax.ShapeDtypeStruct((B, T, D_MODEL), f32),
        compiler_params=_cparams(("parallel", "parallel")),
        name="ffn",
    )(x, mod, norm_g, w_in, w_in, w_out)


def _rope_tables(T):
    pos = jnp.arange(T, dtype=jnp.int32)
    row = (pos // GRID_W).astype(f32)
    col = (pos % GRID_W).astype(f32)
    quarter = HEAD_DIM // 4
    freq = 1.0 / (ROPE_THETA ** (jnp.arange(quarter, dtype=f32) / quarter))
    ang_r = row[:, None] * freq[None, :]
    ang_c = col[:, None] * freq[None, :]
    cos = jnp.concatenate([jnp.cos(ang_r)] * 2 + [jnp.cos(ang_c)] * 2, axis=1)
    sin = jnp.concatenate([-jnp.sin(ang_r), jnp.sin(ang_r), -jnp.sin(ang_c), jnp.sin(ang_c)], axis=1)
    return jnp.concatenate([cos, cos], axis=1), jnp.concatenate([sin, sin], axis=1)


def _block_lora(up):
    z = jnp.zeros_like(up[0])
    return jnp.concatenate([jnp.concatenate([up[0], z], axis=1),
                            jnp.concatenate([z, up[1]], axis=1)], axis=0)


def _layer_params(l, ada_w, ada_b, norm_mix_g, norm_ffn_g, w_in, conv_w, decay_w0, decay_up, iclr_a0,
                  iclr_up, gate_up, k_k, k_a, r_k, ln_x_g, ln_x_b, q_norm_g, k_norm_g, w_out,
                  w_ffn_in, w_ffn_out):
    wi = w_in[l]
    wk = wi[:, 2048:2176]
    wv = wi[:, 2176:2304]
    z64 = jnp.zeros((D_MODEL, HEAD_DIM), wi.dtype)
    dup = lambda w: jnp.concatenate([w[:, 0:64], z64, z64, w[:, 0:64], w[:, 64:128], z64, z64, w[:, 64:128]],
                                    axis=1)
    w_all = jnp.concatenate([wi[:, 0:1536], wi[:, 2304:2688], wi[:, 1536:2048], dup(wk), dup(wv)],
                            axis=1).astype(bf16)
    return dict(
        ada_w=ada_w[l], ada_b=ada_b[l],
        norm_mix_g=norm_mix_g[l].reshape(1, -1), norm_ffn_g=norm_ffn_g[l].reshape(1, -1),
        w_all=w_all, conv_w=conv_w[l],
        wd=_block_lora(decay_up[l]).astype(bf16), w0=decay_w0[l].reshape(1, -1),
        wi=_block_lora(iclr_up[l]).astype(bf16), a0=iclr_a0[l].reshape(1, -1),
        gate_up=gate_up[l].astype(bf16),
        k_k=k_k[l].reshape(1, -1), k_a=k_a[l].reshape(1, -1), r_k=r_k[l].reshape(1, -1),
        ln_g=ln_x_g[l].reshape(1, -1), ln_b=ln_x_b[l].reshape(1, -1),
        qg=jnp.tile(q_norm_g[l], 8).reshape(1, -1), kg=jnp.tile(k_norm_g[l], 8).reshape(1, -1),
        w_out=w_out[l].astype(bf16), w_ffn_in=w_ffn_in[l].astype(bf16),
        w_ffn_out=w_ffn_out[l].astype(bf16))


def _layer(x, c, p, cos, sin, gsum):
    mod = _modulation(c, p["ada_w"], p["ada_b"])
    rkv, lora, q, k, vt = _in_proj(x, mod, p["norm_mix_g"], p["w_all"], cos, sin, p["qg"], p["kg"], gsum)
    r, kr, vr, kk, lw, ic, bonus = _rwkv_prep(rkv, lora, p["conv_w"], p["wd"], p["w0"], p["wi"], p["a0"],
                                              p["k_k"], p["k_a"], p["r_k"], gsum)
    y = _rwkv_scan(r, kr, vr, kk, lw, ic, p["k_a"])
    attn = _attention(q, k, vt)
    x1 = _mix(x, y, bonus, lora, attn, mod, gsum, p["gate_up"], p["w_out"], p["ln_g"], p["ln_b"])
    return _ffn(x1, mod, p["norm_ffn_g"], p["w_ffn_in"], p["w_ffn_out"])


def kernel(x_prompt, x_sample, c_prompt, c_sample, ada_w, ada_b, norm_mix_g, norm_ffn_g, w_in, conv_w,
           decay_w0, decay_up, iclr_a0, iclr_up, gate_up, k_k, k_a, r_k, ln_x_g, ln_x_b, q_norm_g,
           k_norm_g, w_out, w_ffn_in, w_ffn_out):
    depth = ada_w.shape[0]
    params = [_layer_params(l, ada_w, ada_b, norm_mix_g, norm_ffn_g, w_in, conv_w, decay_w0, decay_up,
                            iclr_a0, iclr_up, gate_up, k_k, k_a, r_k, ln_x_g, ln_x_b, q_norm_g,
                            k_norm_g, w_out, w_ffn_in, w_ffn_out) for l in range(depth)]
    head = np.arange(512) // HEAD_DIM
    gsum = jnp.asarray(head[:, None] == head[None, :], dtype=bf16)

    def run_trunk(x, c):
        cos, sin = _rope_tables(x.shape[1])
        for p in params:
            x = _layer(x, c, p, cos, sin, gsum)
        return x

    return (run_trunk(x_prompt, c_prompt), run_trunk(x_sample, c_sample))
```

```python
import functools
import math

import jax
import jax.numpy as jnp
import numpy as np
from jax import lax
from jax.experimental import pallas as pl
from jax.experimental.pallas import tpu as pltpu

f32 = jnp.float32
bf16 = jnp.bfloat16

D_MODEL = 1024
HEAD_DIM = 64
RWKV_DIM = 512
ATTN_DIM = 512
KV_HEADS = 2
LORA = 64
D_FF = 2816
GRID_W = 64
ROPE_THETA = 10000.0
NORM_EPS = 1e-6
QK_EPS = 1e-6
GN_EPS = 64e-5
DECAY_SCALE = math.exp(-0.5)

LANES = 128
VMEM_LIMIT = 48 * 1024 * 1024

ROW_BLOCK = 512
HALO = 16
FFN_ROWS = 512
CHUNK = 64
SCAN_CHUNKS = 4
ATTN_BQ = 2048
ATTN_BK = 1024
ATTN_BUFS = 3
ATTN_QSUB = 512
ATTN_STRIP = 16
LOG2E = 1.4426950408889634
VT_ROWS = 80

C_RKV = 0
C_LORA = 1536
C_Q = 1920
C_K = 2432
C_V = 2688
C_END = 2816


def _cparams(sem):
    return pltpu.CompilerParams(dimension_semantics=sem, vmem_limit_bytes=VMEM_LIMIT)


def _dot(a, b):
    return jnp.dot(a.astype(bf16), b.astype(bf16), preferred_element_type=f32)


def _dot_nt(a, b):
    return lax.dot_general(a.astype(bf16), b.astype(bf16), (((1,), (1,)), ((), ())),
                           preferred_element_type=f32)


def _dot_tn(a, b):
    return lax.dot_general(a.astype(bf16), b.astype(bf16), (((0,), (0,)), ((), ())),
                           preferred_element_type=f32)


def _split2(a):
    hi = a.astype(bf16)
    lo = (a - hi.astype(f32)).astype(bf16)
    return hi, lo


def _dot_exact_rhs(a, g):
    hi, lo = _split2(a)
    return (jnp.dot(hi, g, preferred_element_type=f32) + jnp.dot(lo, g, preferred_element_type=f32))


def _sigmoid(x):
    return 1.0 / (1.0 + jnp.exp(-x))


def _iota(shape, dim):
    return lax.broadcasted_iota(jnp.int32, shape, dim)


def _mod_kernel(c_ref, w_ref, b_ref, o_ref):
    c = c_ref[...]
    s = c * _sigmoid(c)
    sh, sl = _split2(s)
    wh, wl = _split2(w_ref[...])
    acc = jnp.dot(sh, wh, preferred_element_type=f32)
    acc += jnp.dot(sh, wl, preferred_element_type=f32)
    acc += jnp.dot(sl, wh, preferred_element_type=f32)
    o_ref[...] = acc + b_ref[...]


def _modulation(c, ada_w, ada_b):
    B = c.shape[0]
    Bp = max(8, B)
    cp = jnp.pad(c, ((0, Bp - B), (0, 0)))
    n = ada_w.shape[1] // D_MODEL
    out = pl.pallas_call(
        _mod_kernel,
        grid=(n,),
        in_specs=[pl.BlockSpec((Bp, D_MODEL), lambda j: (0, 0)),
                  pl.BlockSpec((D_MODEL, D_MODEL), lambda j: (0, j)),
                  pl.BlockSpec((1, D_MODEL), lambda j: (0, j))],
        out_specs=pl.BlockSpec((Bp, D_MODEL), lambda j: (0, j)),
        out_shape=jax.ShapeDtypeStruct((Bp, n * D_MODEL), f32),
        compiler_params=_cparams(("arbitrary",)),
        name="adaln_mod",
    )(cp, ada_w, ada_b.reshape(1, -1))
    return out[:B].reshape(B, n, D_MODEL)


def _rope(x, cos, sin):
    w = x.shape[1]
    up = pltpu.roll(x, w - 16, 1)
    dn = pltpu.roll(x, 16, 1)
    first = (_iota((1, w), 1) & 16) == 0
    return x * cos + jnp.where(first, up, dn) * sin


def _qk_norm(x, g_ref, gain):
    w = x.shape[1]
    ms = _dot_exact_rhs(x * x, g_ref[0:w, 0:w]) * (1.0 / HEAD_DIM)
    return x * lax.rsqrt(ms + QK_EPS) * gain


def _front_kernel(x_ref, xp_ref, xn_ref, mod_ref, ng_ref, w_ref, cos_ref, sin_ref, qg_ref, kg_ref, gs_ref,
                  cw_ref, wd_ref, w0_ref, wi_ref, a0_ref, kk_ref, ka_ref, rk_ref,
                  r_ref, k_ref, v_ref, kkn_ref, lw_ref, ic_ref, bonus_ref, xg_ref, q_ref, ka_out_ref, vt_ref,
                  vs_ref):
    i = pl.program_id(1)
    n = pl.num_programs(1)
    bm = x_ref.shape[1]
    halo = xp_ref.shape[1]

    def ada_norm(x):
        ms = jnp.mean(x * x, axis=-1, keepdims=True)
        h = x * lax.rsqrt(ms + NORM_EPS) * ng_ref[...]
        return h * (1.0 + mod_ref[0, 1:2, :]) + mod_ref[0, 0:1, :]

    h_cur = ada_norm(x_ref[0])
    h_ext = jnp.concatenate([ada_norm(xp_ref[0]) * jnp.where(i > 0, 1.0, 0.0), h_cur,
                             ada_norm(xn_ref[0]) * jnp.where(i < n - 1, 1.0, 0.0)], axis=0)
    rkv_ext = jnp.dot(h_ext.astype(bf16), w_ref[:, C_RKV:C_LORA], preferred_element_type=f32)
    hb = h_cur.astype(bf16)
    lora = jnp.dot(hb, w_ref[:, C_LORA:C_Q], preferred_element_type=f32)
    q = jnp.dot(hb, w_ref[:, C_Q:C_K], preferred_element_type=f32)
    k = jnp.dot(hb, w_ref[:, C_K:C_V], preferred_element_type=f32)
    vs_ref[...] = jnp.dot(hb, w_ref[:, C_V:C_END], preferred_element_type=f32)

    vt = vs_ref[...].T.astype(bf16)
    pad = jnp.where(_iota((VT_ROWS - HEAD_DIM, vt.shape[1]), 0) == 0, 1.0, 0.0).astype(bf16)
    for g in range(KV_HEADS):
        vt_ref[0, g * VT_ROWS:g * VT_ROWS + HEAD_DIM, :] = vt[g * HEAD_DIM:(g + 1) * HEAD_DIM]
        vt_ref[0, g * VT_ROWS + HEAD_DIM:(g + 1) * VT_ROWS, :] = pad
    cos = cos_ref[...]
    sin = sin_ref[...]
    cos4 = jnp.concatenate([cos] * 4, axis=1)
    sin4 = jnp.concatenate([sin] * 4, axis=1)
    qn = _rope(_qk_norm(q, gs_ref, qg_ref[...]), cos4, sin4)
    q_ref[0] = (qn * (LOG2E * HEAD_DIM ** -0.5)).astype(bf16)
    kn = _rope(_qk_norm(k, gs_ref, kg_ref[...]), cos4[:, 0:256], sin4[:, 0:256])
    half0 = _iota((1, LANES), 1) < HEAD_DIM
    ka_out_ref[0] = jnp.concatenate([jnp.where(keep, kn[:, g * LANES:(g + 1) * LANES], 0.0)
                                     for g in range(KV_HEADS) for keep in (half0, jnp.logical_not(half0))],
                                    axis=1).astype(bf16)

    next_shift = bm + 2 * halo - 1
    outs = (r_ref, k_ref, v_ref)
    rkv = []
    for p in range(3):
        cols = slice(p * 512, (p + 1) * 512)
        ext = rkv_ext[:, cols]
        y = (cw_ref[0:1, cols] * pltpu.roll(ext, 1, 0)[halo:halo + bm]
             + cw_ref[1:2, cols] * ext[halo:halo + bm]
             + cw_ref[2:3, cols] * pltpu.roll(ext, next_shift, 0)[halo:halo + bm])
        outs[p][0] = y
        rkv.append(y)
    r, kr, vr = rkv
    kkh = kr * kk_ref[...]
    kkn_ref[0] = kkh * lax.rsqrt(_dot_exact_rhs(kkh * kkh, gs_ref[...]) + 1e-12)
    dl = _dot(jnp.tanh(lora[:, 0:128]), wd_ref[...]) + w0_ref[...]
    lw = -DECAY_SCALE * _sigmoid(dl)
    lw_ref[0, 0] = lw[:, 0:512]
    lw_ref[1, 0] = lw[:, 512:1024]
    ic = _sigmoid(_dot(lora[:, 128:256], wi_ref[...]) + a0_ref[...])
    ic_ref[0, 0] = ic[:, 0:512]
    ic_ref[1, 0] = ic[:, 512:1024]
    xg_ref[0] = lora[:, 256:384]
    kb = kr * (1.0 + (0.5 * (ic[:, 0:512] + ic[:, 512:1024]) - 1.0) * ka_ref[...])
    bonus_ref[0] = _dot_exact_rhs(r * kb * rk_ref[...], gs_ref[...]) * vr


def _front(x, mod, p, cos, sin, gsum):
    B, T, _ = x.shape
    bm = ROW_BLOCK
    hb = bm // HALO
    nh = T // HALO
    row3 = lambda b, i: (b, i, 0)
    const2 = lambda b, i: (0, 0)
    once = pl.Buffered(1)
    cst = lambda shape: pl.BlockSpec(shape, const2, pipeline_mode=once)
    o512 = jax.ShapeDtypeStruct((B, T, 512), f32)
    o2 = jax.ShapeDtypeStruct((2, B, T, 512), f32)
    dir4 = pl.BlockSpec((2, 1, bm, 512), lambda b, i: (0, b, i, 0))
    return pl.pallas_call(
        _front_kernel,
        grid=(B, T // bm),
        in_specs=[pl.BlockSpec((1, bm, D_MODEL), row3),
                  pl.BlockSpec((1, HALO, D_MODEL), lambda b, i: (b, jnp.maximum(i * hb - 1, 0), 0)),
                  pl.BlockSpec((1, HALO, D_MODEL), lambda b, i: (b, jnp.minimum((i + 1) * hb, nh - 1), 0)),
                  pl.BlockSpec((1, 6, D_MODEL), lambda b, i: (b, 0, 0)),
                  cst((1, D_MODEL)), cst((D_MODEL, C_END)),
                  pl.BlockSpec((bm, LANES), lambda b, i: (i, 0)),
                  pl.BlockSpec((bm, LANES), lambda b, i: (i, 0)),
                  cst((1, 512)), cst((1, 256)), cst((512, 512)),
                  cst((3, 1536)), cst((128, 1024)), cst((1, 1024)), cst((128, 1024)), cst((1, 1024)),
                  cst((1, 512)), cst((1, 512)), cst((1, 512))],
        out_specs=[pl.BlockSpec((1, bm, 512), row3)] * 4 + [dir4, dir4]
        + [pl.BlockSpec((1, bm, 512), row3), pl.BlockSpec((1, bm, LANES), row3),
           pl.BlockSpec((1, bm, 512), row3), pl.BlockSpec((1, bm, 512), row3),
           pl.BlockSpec((1, KV_HEADS * VT_ROWS, bm), lambda b, i: (b, 0, i))],
        out_shape=[o512, o512, o512, o512, o2, o2, o512,
                   jax.ShapeDtypeStruct((B, T, LANES), f32),
                   jax.ShapeDtypeStruct((B, T, 512), bf16),
                   jax.ShapeDtypeStruct((B, T, 512), bf16),
                   jax.ShapeDtypeStruct((B, KV_HEADS * VT_ROWS, T), bf16)],
        scratch_shapes=[pltpu.VMEM((bm, KV_HEADS * HEAD_DIM), f32)],
        compiler_params=_cparams(("parallel", "parallel")),
        name="front",
    )(x, x, x, mod, p["norm_mix_g"], p["w_all"], cos, sin, p["qg"], p["kg"], gsum,
      p["conv_w"], p["wd"], p["w0"], p["wi"], p["a0"], p["k_k"], p["k_a"], p["r_k"])


def _scan_kernel(r_ref, k_ref, v_ref, kk_ref, lw_ref, ic_ref, ka_ref, y_ref, h_ref, *, nc):
    L = CHUNK
    d = pl.program_id(0)
    fwd = d == 0

    @pl.when(pl.program_id(2) == 0)
    def _():
        h_ref[...] = jnp.zeros_like(h_ref)

    sgn = jnp.where(fwd, 1, -1)
    col = _iota((L, LANES), 1)
    row = _iota((L, LANES), 0)
    order = ((col & (L - 1)) - row) * sgn
    left = col < L
    strict_l = (order < 0) & left
    strict_r = (order < 0) & jnp.logical_not(left)
    incl = order <= 0
    eye_r = jnp.where(col - L == row, 1.0, 0.0)
    right_f = jnp.where(left, 0.0, 1.0)
    incl_bf = jnp.where(order[:, 0:L] <= 0, 1.0, 0.0).astype(bf16)
    lane = _iota((1, LANES), 1)
    head_m = (jnp.where(lane < L, 1.0, 0.0), jnp.where(lane < L, 0.0, 1.0))
    blockdiag = (_iota((LANES, LANES), 0) < L) == (_iota((LANES, LANES), 1) < L)
    eye128 = _iota((LANES, LANES), 0) == _iota((LANES, LANES), 1)
    z128 = jnp.zeros((L, LANES), f32)
    z256 = jnp.zeros((L, 2 * LANES), f32)
    ka = ka_ref[...]
    pairs = [(i, p) for i in range(nc) for p in range(4)]
    units = [(i, p, hh) for (i, p) in pairs for hh in range(2)]

    rows_of, ch = [], []
    for i in range(nc):
        ci = jnp.where(fwd, i, nc - 1 - i)
        rows = pl.ds(pl.multiple_of(ci * L, L), L)
        rows_of.append(rows)
        lw = lw_ref[0, 0, rows, :]
        ic = ic_ref[0, 0, rows, :]
        kk = kk_ref[0, rows, :]
        kd = k_ref[0, rows, :] * (1.0 + (ic - 1.0) * ka)
        b = kk * ic
        l1 = lw.astype(bf16)
        e1 = lw - l1.astype(f32)
        l2 = e1.astype(bf16)
        l3 = (e1 - l2.astype(f32)).astype(bf16)
        cum = (jnp.dot(incl_bf, l1, preferred_element_type=f32)
               + jnp.dot(incl_bf, l2, preferred_element_type=f32)
               + jnp.dot(incl_bf, l3, preferred_element_type=f32))
        tot = jnp.where(fwd, cum[L - 1:L, :], cum[0:1, :])
        p_inv = jnp.exp(-cum)
        p_end = jnp.exp(tot - cum)
        ch.append(dict(
            v=v_ref[0, rows, :], p_tot=jnp.exp(tot),
            rt=r_ref[0, rows, :] * jnp.exp(cum), at=-kk * jnp.exp(cum - lw),
            bt=b * p_inv, kt=kd * p_inv, bh=b * p_end, kh=kd * p_end))

    def sl(i, p, name):
        return ch[i][name][:, p * LANES:(p + 1) * LANES]

    am, vm, g = {}, {}, {}
    for (i, p) in pairs:
        z = jnp.concatenate([sl(i, p, "bt"), sl(i, p, "kt")], axis=0).astype(bf16)
        for hh in range(2):
            u = (i, p, hh)
            am[u] = sl(i, p, "at") * head_m[hh]
            vm[u] = sl(i, p, "v") * head_m[hh]
            xm = jnp.concatenate([am[u], sl(i, p, "rt") * head_m[hh]], axis=0)
            g[u] = _dot_nt(xm, z)
    pk, w1 = {}, {}
    for u in units:
        gt = g[u][0:L]
        pk[u] = jnp.where(strict_l, gt, 0.0) + eye_r
        w1[u] = _dot(jnp.where(strict_r, gt, 0.0), jnp.concatenate([z128, vm[u]], axis=0))
    for _ in range(6):
        for u in units:
            prod = _dot(pk[u], jnp.concatenate([pk[u], z128], axis=0))
            pk[u] = prod + pk[u] * right_f
    tu, ry = {}, {}
    for u in units:
        tu[u] = _dot(pk[u], jnp.concatenate([z256, jnp.concatenate([am[u], w1[u]], axis=1)], axis=0))
    for u in units:
        lhs = jnp.where(incl, g[u][L:2 * L], 0.0)
        ry[u] = _dot(lhs, jnp.concatenate([tu[u], jnp.concatenate([z128, vm[u]], axis=1)], axis=0))
    r_pair, y_pair, m_p, c_p, p_col = {}, {}, {}, {}, {}
    for (i, p) in pairs:
        u0, u1 = (i, p, 0), (i, p, 1)
        t2 = tu[u0] + tu[u1]
        r2 = ry[u0] + ry[u1]
        r_pair[i, p] = sl(i, p, "rt") + r2[:, 0:LANES]
        y_pair[i, p] = r2[:, LANES:]
        zl = jnp.concatenate([sl(i, p, "bh"), sl(i, p, "kh")], axis=0)
        zr = jnp.concatenate([t2, jnp.concatenate([z128, sl(i, p, "v")], axis=1)], axis=0)
        mc = _dot_tn(zl, zr)
        m_p[i, p] = jnp.where(blockdiag, mc[:, 0:LANES], 0.0)
        c_p[i, p] = jnp.where(blockdiag, mc[:, LANES:], 0.0)
        p_col[i, p] = jnp.sum(jnp.where(eye128, sl(i, p, "p_tot"), 0.0), axis=1, keepdims=True)
    hs = [h_ref[p] for p in range(4)]
    for i in range(nc):
        for p in range(4):
            sd = _dot(jnp.concatenate([r_pair[i, p], m_p[i, p]], axis=0), hs[p])
            y_ref[0, 0, rows_of[i], p * LANES:(p + 1) * LANES] = sd[0:L] + y_pair[i, p]
            hs[p] = p_col[i, p] * hs[p] + sd[L:] + c_p[i, p]
    for p in range(4):
        h_ref[p] = hs[p]


def _rwkv_scan(r, k, v, kk, lw, ic, k_a):
    B, T, _ = r.shape
    nc = SCAN_CHUNKS
    lb = nc * CHUNK
    ns = T // lb

    def blk(d, b, s):
        return jnp.where(d == 0, s, ns - 1 - s)

    shared = pl.BlockSpec((1, lb, 512), lambda d, b, s: (b, blk(d, b, s), 0))
    perdir = pl.BlockSpec((1, 1, lb, 512), lambda d, b, s: (d, b, blk(d, b, s), 0))
    return pl.pallas_call(
        functools.partial(_scan_kernel, nc=nc),
        grid=(2, B, ns),
        in_specs=[shared, shared, shared, shared, perdir, perdir,
                  pl.BlockSpec((1, 512), lambda d, b, s: (0, 0))],
        out_specs=perdir,
        out_shape=jax.ShapeDtypeStruct((2, B, T, 512), f32),
        scratch_shapes=[pltpu.VMEM((4, LANES, LANES), f32)],
        compiler_params=_cparams(("arbitrary", "arbitrary", "arbitrary")),
        name="rwkv_scan",
    )(r, k, v, kk, lw, ic, k_a)


def _attn_kernel(q_ref, k_ref, vt_ref, o_ref, m_ref, acc_ref, *scr):
    ki = pl.program_id(3)
    bk = k_ref.shape[1]

    @pl.when(ki == 0)
    def _():
        m_ref[...] = jnp.full_like(m_ref, -1e30)
        acc_ref[...] = jnp.zeros_like(acc_ref)

    bq = q_ref.shape[1]
    units = [(j, hh, c0) for c0 in range(0, bq, ATTN_QSUB) for j in range(2) for hh in range(2)]
    n = len(units)
    nb = len(scr) // 2
    s_refs = [scr[u % nb] for u in range(n)]
    p_refs = [scr[nb + u % nb] for u in range(n)]
    first = ki == 0
    m_old, m_new = [None] * n, [None] * n

    def scores(u):
        j, hh, c0 = units[u]
        s_refs[u][...] = lax.dot_general(k_ref[0, :, hh * LANES:(hh + 1) * LANES],
                                         q_ref[0, c0:c0 + ATTN_QSUB, j * LANES:(j + 1) * LANES],
                                         (((1,), (1,)), ((), ())), preferred_element_type=f32)

    def softmax(u):
        j, hh, c0 = units[u]
        m_old[u] = m_ref[2 * j + hh, :, c0:c0 + ATTN_QSUB]
        c = jnp.where(first, 0.0, m_old[u])
        smax = jnp.max(s_refs[u][...], axis=0, keepdims=True)
        t = jnp.maximum(jnp.where(first, -1e30, 0.0), smax - c).astype(bf16)
        m_new[u] = c + t.astype(f32)
        for r0 in range(0, bk, ATTN_STRIP):
            rs = slice(r0, r0 + ATTN_STRIP)
            p_refs[u][rs, :] = jnp.exp2((s_refs[u][rs, :] - c).astype(bf16) - t)

    def update(u):
        j, hh, c0 = units[u]
        i = 2 * j + hh
        pv = jnp.dot(vt_ref[0], p_refs[u][...], preferred_element_type=f32)
        acc_ref[i, :, c0:c0 + ATTN_QSUB] = (acc_ref[i, :, c0:c0 + ATTN_QSUB]
                                            * jnp.exp2(m_old[u] - m_new[u]) + pv)
        m_ref[i, :, c0:c0 + ATTN_QSUB] = m_new[u]

    scores(0)
    for u in range(n):
        if u + 1 < n:
            scores(u + 1)
        softmax(u)
        if u > 0:
            update(u - 1)
    update(n - 1)

    @pl.when(ki == pl.num_programs(3) - 1)
    def _():
        for j in range(2):
            o = jnp.concatenate([acc_ref[i, 0:HEAD_DIM, :] * (1.0 / acc_ref[i, HEAD_DIM:HEAD_DIM + 1, :])
                                 for i in (2 * j, 2 * j + 1)], axis=0)
            o_ref[0, :, j * LANES:(j + 1) * LANES] = o.T.astype(o_ref.dtype)


def _attention(q, k, vt):
    B, T, _ = q.shape
    bq, bk = ATTN_BQ, ATTN_BK
    nb = ATTN_BUFS
    return pl.pallas_call(
        _attn_kernel,
        grid=(B, KV_HEADS, T // bq, T // bk),
        in_specs=[pl.BlockSpec((1, bq, 256), lambda b, g, qi, ki: (b, qi, g)),
                  pl.BlockSpec((1, bk, 256), lambda b, g, qi, ki: (b, ki, g)),
                  pl.BlockSpec((1, VT_ROWS, bk), lambda b, g, qi, ki: (b, g, ki))],
        out_specs=pl.BlockSpec((1, bq, 256), lambda b, g, qi, ki: (b, qi, g)),
        out_shape=jax.ShapeDtypeStruct((B, T, ATTN_DIM), bf16),
        scratch_shapes=[pltpu.VMEM((4, 1, bq), f32), pltpu.VMEM((4, VT_ROWS, bq), f32)]
        + [pltpu.VMEM((bk, ATTN_QSUB), f32)] * nb + [pltpu.VMEM((bk, ATTN_QSUB), bf16)] * nb,
        compiler_params=_cparams(("parallel", "parallel", "parallel", "arbitrary")),
        name="gqa_attention",
    )(q, k, vt)


def _mix_kernel(x_ref, y_ref, bonus_ref, xg_ref, attn_ref, mod_ref, gs_ref, gup_ref, wo_ref, lng_ref,
                lnb_ref, o_ref):
    gs = gs_ref[...]
    y = y_ref[0, 0] + y_ref[1, 0]
    mu = _dot_exact_rhs(y, gs) * (1.0 / HEAD_DIM)
    yc = y - mu
    var = _dot_exact_rhs(yc * yc, gs) * (1.0 / HEAD_DIM)
    yn = yc * lax.rsqrt(var + GN_EPS) * lng_ref[...] + lnb_ref[...]
    g = _dot(_sigmoid(xg_ref[0]), gup_ref[...])
    rw = ((yn + bonus_ref[0]) * g).astype(bf16)
    mix = (jnp.dot(rw, wo_ref[0:RWKV_DIM, :], preferred_element_type=f32)
           + jnp.dot(attn_ref[0], wo_ref[RWKV_DIM:, :], preferred_element_type=f32))
    o_ref[0] = x_ref[0] + mod_ref[0, 2:3, :] * mix


def _mix(x, y, bonus, xg, attn, mod, gsum, gate_up, w_out, ln_g, ln_b):
    B, T, _ = x.shape
    bm = ROW_BLOCK
    row3 = lambda b, i: (b, i, 0)
    dir4 = lambda b, i: (0, b, i, 0)
    const2 = lambda b, i: (0, 0)
    vec512 = pl.BlockSpec((1, 512), const2)
    return pl.pallas_call(
        _mix_kernel,
        grid=(B, T // bm),
        in_specs=[pl.BlockSpec((1, bm, D_MODEL), row3),
                  pl.BlockSpec((2, 1, bm, 512), dir4),
                  pl.BlockSpec((1, bm, 512), row3),
                  pl.BlockSpec((1, bm, LANES), row3),
                  pl.BlockSpec((1, bm, 512), row3),
                  pl.BlockSpec((1, 6, D_MODEL), lambda b, i: (b, 0, 0)),
                  pl.BlockSpec((512, 512), const2),
                  pl.BlockSpec((128, 512), const2),
                  pl.BlockSpec((D_MODEL, D_MODEL), const2),
                  vec512, vec512],
        out_specs=pl.BlockSpec((1, bm, D_MODEL), row3),
        out_shape=jax.ShapeDtypeStruct((B, T, D_MODEL), f32),
        compiler_params=_cparams(("parallel", "parallel")),
        name="mix_out",
    )(x, y, bonus, xg, attn, mod, gsum, gate_up, w_out, ln_g, ln_b)


def _ffn_kernel(x_ref, mod_ref, ng_ref, wg_ref, wu_ref, wo_ref, o_ref):
    x = x_ref[0]
    ms = jnp.mean(x * x, axis=-1, keepdims=True)
    h = x * lax.rsqrt(ms + NORM_EPS) * ng_ref[...]
    h = (h * (1.0 + mod_ref[0, 4:5, :]) + mod_ref[0, 3:4, :]).astype(bf16)
    gt = jnp.dot(h, wg_ref[...], preferred_element_type=f32)
    up = jnp.dot(h, wu_ref[...], preferred_element_type=f32)
    act = (gt * _sigmoid(gt) * up).astype(bf16)
    o_ref[0] = x + mod_ref[0, 5:6, :] * jnp.dot(act, wo_ref[...], preferred_element_type=f32)


def _ffn(x, mod, norm_g, w_in, w_out):
    B, T, _ = x.shape
    bm = FFN_ROWS
    row3 = lambda b, i: (b, i, 0)
    resident = pl.Buffered(1)
    return pl.pallas_call(
        _ffn_kernel,
        grid=(B, T // bm),
        in_specs=[pl.BlockSpec((1, bm, D_MODEL), row3),
                  pl.BlockSpec((1, 6, D_MODEL), lambda b, i: (b, 0, 0)),
                  pl.BlockSpec((1, D_MODEL), lambda b, i: (0, 0)),
                  pl.BlockSpec((D_MODEL, D_FF), lambda b, i: (0, 0), pipeline_mode=resident),
                  pl.BlockSpec((D_MODEL, D_FF), lambda b, i: (0, 1), pipeline_mode=resident),
                  pl.BlockSpec((D_FF, D_MODEL), lambda b, i: (0, 0), pipeline_mode=resident)],
        out_specs=pl.BlockSpec((1, bm, D_MODEL), row3),
        out_shape=jax.ShapeDtypeStruct((B, T, D_MODEL), f32),
        compiler_params=_cparams(("parallel", "parallel")),
        name="ffn",
    )(x, mod, norm_g, w_in, w_in, w_out)


def _rope_tables(T):
    pos = jnp.arange(T, dtype=jnp.int32)
    row = (pos // GRID_W).astype(f32)
    col = (pos % GRID_W).astype(f32)
    quarter = HEAD_DIM // 4
    freq = 1.0 / (ROPE_THETA ** (jnp.arange(quarter, dtype=f32) / quarter))
    ang_r = row[:, None] * freq[None, :]
    ang_c = col[:, None] * freq[None, :]
    cos = jnp.concatenate([jnp.cos(ang_r)] * 2 + [jnp.cos(ang_c)] * 2, axis=1)
    sin = jnp.concatenate([-jnp.sin(ang_r), jnp.sin(ang_r), -jnp.sin(ang_c), jnp.sin(ang_c)], axis=1)
    return jnp.concatenate([cos, cos], axis=1), jnp.concatenate([sin, sin], axis=1)


def _block_lora(up):
    z = jnp.zeros_like(up[0])
    return jnp.concatenate([jnp.concatenate([up[0], z], axis=1),
                            jnp.concatenate([z, up[1]], axis=1)], axis=0)


def _layer_params(l, ada_w, ada_b, norm_mix_g, norm_ffn_g, w_in, conv_w, decay_w0, decay_up, iclr_a0,
                  iclr_up, gate_up, k_k, k_a, r_k, ln_x_g, ln_x_b, q_norm_g, k_norm_g, w_out,
                  w_ffn_in, w_ffn_out):
    wi = w_in[l]
    wk = wi[:, 2048:2176]
    wv = wi[:, 2176:2304]
    wk2 = jnp.concatenate([wk[:, 0:64], wk[:, 0:64], wk[:, 64:128], wk[:, 64:128]], axis=1)
    w_all = jnp.concatenate([wi[:, 0:1536], wi[:, 2304:2688], wi[:, 1536:2048], wk2, wv], axis=1).astype(bf16)
    return dict(
        ada_w=ada_w[l], ada_b=ada_b[l],
        norm_mix_g=norm_mix_g[l].reshape(1, -1), norm_ffn_g=norm_ffn_g[l].reshape(1, -1),
        w_all=w_all, conv_w=conv_w[l],
        wd=_block_lora(decay_up[l]).astype(bf16), w0=decay_w0[l].reshape(1, -1),
        wi=_block_lora(iclr_up[l]).astype(bf16), a0=iclr_a0[l].reshape(1, -1),
        gate_up=gate_up[l].astype(bf16),
        k_k=k_k[l].reshape(1, -1), k_a=k_a[l].reshape(1, -1), r_k=r_k[l].reshape(1, -1),
        ln_g=ln_x_g[l].reshape(1, -1), ln_b=ln_x_b[l].reshape(1, -1),
        qg=jnp.tile(q_norm_g[l], 8).reshape(1, -1), kg=jnp.tile(k_norm_g[l], 4).reshape(1, -1),
        w_out=w_out[l].astype(bf16), w_ffn_in=w_ffn_in[l].astype(bf16),
        w_ffn_out=w_ffn_out[l].astype(bf16))


def _layer(x, c, p, cos, sin, gsum):
    mod = _modulation(c, p["ada_w"], p["ada_b"])
    r, kr, vr, kk, lw, ic, bonus, xg, q, k, vt = _front(x, mod, p, cos, sin, gsum)
    y = _rwkv_scan(r, kr, vr, kk, lw, ic, p["k_a"])
    attn = _attention(q, k, vt)
    x1 = _mix(x, y, bonus, xg, attn, mod, gsum, p["gate_up"], p["w_out"], p["ln_g"], p["ln_b"])
    return _ffn(x1, mod, p["norm_ffn_g"], p["w_ffn_in"], p["w_ffn_out"])


def kernel(x_prompt, x_sample, c_prompt, c_sample, ada_w, ada_b, norm_mix_g, norm_ffn_g, w_in, conv_w,
           decay_w0, decay_up, iclr_a0, iclr_up, gate_up, k_k, k_a, r_k, ln_x_g, ln_x_b, q_norm_g,
           k_norm_g, w_out, w_ffn_in, w_ffn_out):
    depth = ada_w.shape[0]
    params = [_layer_params(l, ada_w, ada_b, norm_mix_g, norm_ffn_g, w_in, conv_w, decay_w0, decay_up,
                            iclr_a0, iclr_up, gate_up, k_k, k_a, r_k, ln_x_g, ln_x_b, q_norm_g,
                            k_norm_g, w_out, w_ffn_in, w_ffn_out) for l in range(depth)]
    head = np.arange(512) // HEAD_DIM
    gsum = jnp.asarray(head[:, None] == head[None, :], dtype=bf16)

    def run_trunk(x, c):
        cos, sin = _rope_tables(x.shape[1])
        for p in params:
            x = _layer(x, c, p, cos, sin, gsum)
        return x

    return (run_trunk(x_prompt, c_prompt), run_trunk(x_sample, c_sample))
```

```python
import functools
import math

import jax
import jax.numpy as jnp
import numpy as np
from jax import lax
from jax.experimental import pallas as pl
from jax.experimental.pallas import tpu as pltpu

f32 = jnp.float32
bf16 = jnp.bfloat16

D_MODEL = 1024
HEAD_DIM = 64
RWKV_DIM = 512
ATTN_DIM = 512
KV_HEADS = 2
LORA = 64
D_FF = 2816
GRID_W = 64
ROPE_THETA = 10000.0
NORM_EPS = 1e-6
QK_EPS = 1e-6
GN_EPS = 64e-5
DECAY_SCALE = math.exp(-0.5)

LANES = 128
VMEM_LIMIT = 48 * 1024 * 1024
BACK_VMEM_LIMIT = 56 * 1024 * 1024

ROW_BLOCK = 512
HALO = 16
FFN_ROWS = 512
CHUNK = 64
SCAN_CHUNKS = 4
ATTN_BQ = 2048
ATTN_BK = 2048
ATTN_BUFS = 3
ATTN_QSUB = 512
ATTN_STRIP = 16
LOG2E = 1.4426950408889634
VT_ROWS = 80

C_RKV = 0
C_LORA = 1536
C_Q = 1920
C_K = 2432
C_V = 2688
C_END = 2816


def _cparams(sem, vmem_limit=None):
    return pltpu.CompilerParams(dimension_semantics=sem, vmem_limit_bytes=vmem_limit or VMEM_LIMIT)


def _dot(a, b):
    return jnp.dot(a.astype(bf16), b.astype(bf16), preferred_element_type=f32)


def _dot_nt(a, b):
    return lax.dot_general(a.astype(bf16), b.astype(bf16), (((1,), (1,)), ((), ())),
                           preferred_element_type=f32)


def _dot_tn(a, b):
    return lax.dot_general(a.astype(bf16), b.astype(bf16), (((0,), (0,)), ((), ())),
                           preferred_element_type=f32)


def _split2(a):
    hi = a.astype(bf16)
    lo = (a - hi.astype(f32)).astype(bf16)
    return hi, lo


def _dot_exact_rhs(a, g):
    hi, lo = _split2(a)
    return (jnp.dot(hi, g, preferred_element_type=f32) + jnp.dot(lo, g, preferred_element_type=f32))


def _sigmoid(x):
    return 1.0 / (1.0 + jnp.exp(-x))


def _iota(shape, dim):
    return lax.broadcasted_iota(jnp.int32, shape, dim)


def _mod_kernel(c_ref, w_ref, b_ref, o_ref):
    c = c_ref[...]
    s = c * _sigmoid(c)
    sh, sl = _split2(s)
    wh, wl = _split2(w_ref[...])
    acc = jnp.dot(sh, wh, preferred_element_type=f32)
    acc += jnp.dot(sh, wl, preferred_element_type=f32)
    acc += jnp.dot(sl, wh, preferred_element_type=f32)
    o_ref[...] = acc + b_ref[...]


def _modulation(c, ada_w, ada_b):
    B = c.shape[0]
    Bp = max(8, B)
    cp = jnp.pad(c, ((0, Bp - B), (0, 0)))
    n = ada_w.shape[1] // D_MODEL
    out = pl.pallas_call(
        _mod_kernel,
        grid=(n,),
        in_specs=[pl.BlockSpec((Bp, D_MODEL), lambda j: (0, 0)),
                  pl.BlockSpec((D_MODEL, D_MODEL), lambda j: (0, j)),
                  pl.BlockSpec((1, D_MODEL), lambda j: (0, j))],
        out_specs=pl.BlockSpec((Bp, D_MODEL), lambda j: (0, j)),
        out_shape=jax.ShapeDtypeStruct((Bp, n * D_MODEL), f32),
        compiler_params=_cparams(("arbitrary",)),
        name="adaln_mod",
    )(cp, ada_w, ada_b.reshape(1, -1))
    return out[:B].reshape(B, n, D_MODEL)


def _rope(x, cos, sin):
    w = x.shape[1]
    up = pltpu.roll(x, w - 16, 1)
    dn = pltpu.roll(x, 16, 1)
    first = (_iota((1, w), 1) & 16) == 0
    return x * cos + jnp.where(first, up, dn) * sin


def _qk_norm(x, g_ref, gain):
    w = x.shape[1]
    ms = _dot_exact_rhs(x * x, g_ref[0:w, 0:w]) * (1.0 / HEAD_DIM)
    return x * lax.rsqrt(ms + QK_EPS) * gain


def _front_kernel(x_ref, xp_ref, xn_ref, mod_ref, ng_ref, w_ref, cos_ref, sin_ref, qg_ref, kg_ref, gs_ref,
                  cw_ref, wd_ref, w0_ref, wi_ref, a0_ref, kk_ref, ka_ref, rk_ref,
                  r_ref, k_ref, v_ref, kkn_ref, lw_ref, ic_ref, bonus_ref, xg_ref, q_ref, ka_out_ref, vt_ref,
                  vs_ref):
    i = pl.program_id(1)
    n = pl.num_programs(1)
    bm = x_ref.shape[1]
    halo = xp_ref.shape[1]

    def ada_norm(x):
        ms = jnp.mean(x * x, axis=-1, keepdims=True)
        h = x * lax.rsqrt(ms + NORM_EPS) * ng_ref[...]
        return h * (1.0 + mod_ref[0, 1:2, :]) + mod_ref[0, 0:1, :]

    h_cur = ada_norm(x_ref[0])
    h_ext = jnp.concatenate([ada_norm(xp_ref[0]) * jnp.where(i > 0, 1.0, 0.0), h_cur,
                             ada_norm(xn_ref[0]) * jnp.where(i < n - 1, 1.0, 0.0)], axis=0)
    rkv_ext = jnp.dot(h_ext.astype(bf16), w_ref[:, C_RKV:C_LORA], preferred_element_type=f32)
    hb = h_cur.astype(bf16)
    lora = jnp.dot(hb, w_ref[:, C_LORA:C_Q], preferred_element_type=f32)
    q = jnp.dot(hb, w_ref[:, C_Q:C_K], preferred_element_type=f32)
    k = jnp.dot(hb, w_ref[:, C_K:C_V], preferred_element_type=f32)
    vs_ref[...] = jnp.dot(hb, w_ref[:, C_V:C_END], preferred_element_type=f32)

    vt = vs_ref[...].T.astype(bf16)
    pad = jnp.where(_iota((VT_ROWS - HEAD_DIM, vt.shape[1]), 0) == 0, 1.0, 0.0).astype(bf16)
    for g in range(KV_HEADS):
        vt_ref[0, g * VT_ROWS:g * VT_ROWS + HEAD_DIM, :] = vt[g * HEAD_DIM:(g + 1) * HEAD_DIM]
        vt_ref[0, g * VT_ROWS + HEAD_DIM:(g + 1) * VT_ROWS, :] = pad
    cos = cos_ref[...]
    sin = sin_ref[...]
    cos4 = jnp.concatenate([cos] * 4, axis=1)
    sin4 = jnp.concatenate([sin] * 4, axis=1)
    qn = _rope(_qk_norm(q, gs_ref, qg_ref[...]), cos4, sin4)
    q_ref[0] = (qn * (LOG2E * HEAD_DIM ** -0.5)).astype(bf16)
    kn = _rope(_qk_norm(k, gs_ref, kg_ref[...]), cos4[:, 0:256], sin4[:, 0:256])
    half0 = _iota((1, LANES), 1) < HEAD_DIM
    ka_out_ref[0] = jnp.concatenate([jnp.where(keep, kn[:, g * LANES:(g + 1) * LANES], 0.0)
                                     for g in range(KV_HEADS) for keep in (half0, jnp.logical_not(half0))],
                                    axis=1).astype(bf16)

    next_shift = bm + 2 * halo - 1
    outs = (r_ref, k_ref, v_ref)
    rkv = []
    for p in range(3):
        cols = slice(p * 512, (p + 1) * 512)
        ext = rkv_ext[:, cols]
        y = (cw_ref[0:1, cols] * pltpu.roll(ext, 1, 0)[halo:halo + bm]
             + cw_ref[1:2, cols] * ext[halo:halo + bm]
             + cw_ref[2:3, cols] * pltpu.roll(ext, next_shift, 0)[halo:halo + bm])
        outs[p][0] = y
        rkv.append(y)
    r, kr, vr = rkv
    kkh = kr * kk_ref[...]
    kkn_ref[0] = kkh * lax.rsqrt(_dot_exact_rhs(kkh * kkh, gs_ref[...]) + 1e-12)
    dl = _dot(jnp.tanh(lora[:, 0:128]), wd_ref[...]) + w0_ref[...]
    lw = -DECAY_SCALE * _sigmoid(dl)
    lw_ref[0, 0] = lw[:, 0:512]
    lw_ref[1, 0] = lw[:, 512:1024]
    ic = _sigmoid(_dot(lora[:, 128:256], wi_ref[...]) + a0_ref[...])
    ic_ref[0, 0] = ic[:, 0:512]
    ic_ref[1, 0] = ic[:, 512:1024]
    xg_ref[0] = lora[:, 256:384]
    kb = kr * (1.0 + (0.5 * (ic[:, 0:512] + ic[:, 512:1024]) - 1.0) * ka_ref[...])
    bonus_ref[0] = _dot_exact_rhs(r * kb * rk_ref[...], gs_ref[...]) * vr


def _front(x, mod, p, cos, sin, gsum):
    B, T, _ = x.shape
    bm = ROW_BLOCK
    hb = bm // HALO
    nh = T // HALO
    row3 = lambda b, i: (b, i, 0)
    const2 = lambda b, i: (0, 0)
    once = pl.Buffered(1)
    cst = lambda shape: pl.BlockSpec(shape, const2, pipeline_mode=once)
    o512 = jax.ShapeDtypeStruct((B, T, 512), f32)
    o2 = jax.ShapeDtypeStruct((2, B, T, 512), f32)
    dir4 = pl.BlockSpec((2, 1, bm, 512), lambda b, i: (0, b, i, 0))
    return pl.pallas_call(
        _front_kernel,
        grid=(B, T // bm),
        in_specs=[pl.BlockSpec((1, bm, D_MODEL), row3),
                  pl.BlockSpec((1, HALO, D_MODEL), lambda b, i: (b, jnp.maximum(i * hb - 1, 0), 0)),
                  pl.BlockSpec((1, HALO, D_MODEL), lambda b, i: (b, jnp.minimum((i + 1) * hb, nh - 1), 0)),
                  pl.BlockSpec((1, 6, D_MODEL), lambda b, i: (b, 0, 0)),
                  cst((1, D_MODEL)), cst((D_MODEL, C_END)),
                  pl.BlockSpec((bm, LANES), lambda b, i: (i, 0)),
                  pl.BlockSpec((bm, LANES), lambda b, i: (i, 0)),
                  cst((1, 512)), cst((1, 256)), cst((512, 512)),
                  cst((3, 1536)), cst((128, 1024)), cst((1, 1024)), cst((128, 1024)), cst((1, 1024)),
                  cst((1, 512)), cst((1, 512)), cst((1, 512))],
        out_specs=[pl.BlockSpec((1, bm, 512), row3)] * 4 + [dir4, dir4]
        + [pl.BlockSpec((1, bm, 512), row3), pl.BlockSpec((1, bm, LANES), row3),
           pl.BlockSpec((1, bm, 512), row3), pl.BlockSpec((1, bm, 512), row3),
           pl.BlockSpec((1, KV_HEADS * VT_ROWS, bm), lambda b, i: (b, 0, i))],
        out_shape=[o512, o512, o512, o512, o2, o2, o512,
                   jax.ShapeDtypeStruct((B, T, LANES), f32),
                   jax.ShapeDtypeStruct((B, T, 512), bf16),
                   jax.ShapeDtypeStruct((B, T, 512), bf16),
                   jax.ShapeDtypeStruct((B, KV_HEADS * VT_ROWS, T), bf16)],
        scratch_shapes=[pltpu.VMEM((bm, KV_HEADS * HEAD_DIM), f32)],
        compiler_params=_cparams(("parallel", "parallel")),
        name="front",
    )(x, x, x, mod, p["norm_mix_g"], p["w_all"], cos, sin, p["qg"], p["kg"], gsum,
      p["conv_w"], p["wd"], p["w0"], p["wi"], p["a0"], p["k_k"], p["k_a"], p["r_k"])


def _scan_kernel(r_ref, k_ref, v_ref, kk_ref, lw_ref, ic_ref, ka_ref, y_ref, h_ref, *, nc):
    L = CHUNK
    d = pl.program_id(0)
    fwd = d == 0

    @pl.when(pl.program_id(2) == 0)
    def _():
        h_ref[...] = jnp.zeros_like(h_ref)

    sgn = jnp.where(fwd, 1, -1)
    col = _iota((L, LANES), 1)
    row = _iota((L, LANES), 0)
    order = ((col & (L - 1)) - row) * sgn
    left = col < L
    strict_l = (order < 0) & left
    strict_r = (order < 0) & jnp.logical_not(left)
    incl = order <= 0
    eye_r = jnp.where(col - L == row, 1.0, 0.0)
    right_f = jnp.where(left, 0.0, 1.0)
    incl_bf = jnp.where(order[:, 0:L] <= 0, 1.0, 0.0).astype(bf16)
    lane = _iota((1, LANES), 1)
    head_m = (jnp.where(lane < L, 1.0, 0.0), jnp.where(lane < L, 0.0, 1.0))
    blockdiag = (_iota((LANES, LANES), 0) < L) == (_iota((LANES, LANES), 1) < L)
    eye128 = _iota((LANES, LANES), 0) == _iota((LANES, LANES), 1)
    z128 = jnp.zeros((L, LANES), f32)
    z256 = jnp.zeros((L, 2 * LANES), f32)
    ka = ka_ref[...]
    pairs = [(i, p) for i in range(nc) for p in range(4)]
    units = [(i, p, hh) for (i, p) in pairs for hh in range(2)]

    rows_of, ch = [], []
    for i in range(nc):
        ci = jnp.where(fwd, i, nc - 1 - i)
        rows = pl.ds(pl.multiple_of(ci * L, L), L)
        rows_of.append(rows)
        lw = lw_ref[0, 0, rows, :]
        ic = ic_ref[0, 0, rows, :]
        kk = kk_ref[0, rows, :]
        kd = k_ref[0, rows, :] * (1.0 + (ic - 1.0) * ka)
        b = kk * ic
        l1 = lw.astype(bf16)
        e1 = lw - l1.astype(f32)
        l2 = e1.astype(bf16)
        l3 = (e1 - l2.astype(f32)).astype(bf16)
        cum = (jnp.dot(incl_bf, l1, preferred_element_type=f32)
               + jnp.dot(incl_bf, l2, preferred_element_type=f32)
               + jnp.dot(incl_bf, l3, preferred_element_type=f32))
        tot = jnp.where(fwd, cum[L - 1:L, :], cum[0:1, :])
        p_inv = jnp.exp(-cum)
        p_end = jnp.exp(tot - cum)
        ch.append(dict(
            v=v_ref[0, rows, :], p_tot=jnp.exp(tot),
            rt=r_ref[0, rows, :] * jnp.exp(cum), at=-kk * jnp.exp(cum - lw),
            bt=b * p_inv, kt=kd * p_inv, bh=b * p_end, kh=kd * p_end))

    def sl(i, p, name):
        return ch[i][name][:, p * LANES:(p + 1) * LANES]

    am, vm, g = {}, {}, {}
    for (i, p) in pairs:
        z = jnp.concatenate([sl(i, p, "bt"), sl(i, p, "kt")], axis=0).astype(bf16)
        for hh in range(2):
            u = (i, p, hh)
            am[u] = sl(i, p, "at") * head_m[hh]
            vm[u] = sl(i, p, "v") * head_m[hh]
            xm = jnp.concatenate([am[u], sl(i, p, "rt") * head_m[hh]], axis=0)
            g[u] = _dot_nt(xm, z)
    pk, w1 = {}, {}
    for u in units:
        gt = g[u][0:L]
        pk[u] = jnp.where(strict_l, gt, 0.0) + eye_r
        w1[u] = _dot(jnp.where(strict_r, gt, 0.0), jnp.concatenate([z128, vm[u]], axis=0))
    for _ in range(6):
        for u in units:
            prod = _dot(pk[u], jnp.concatenate([pk[u], z128], axis=0))
            pk[u] = prod + pk[u] * right_f
    tu, ry = {}, {}
    for u in units:
        tu[u] = _dot(pk[u], jnp.concatenate([z256, jnp.concatenate([am[u], w1[u]], axis=1)], axis=0))
    for u in units:
        lhs = jnp.where(incl, g[u][L:2 * L], 0.0)
        ry[u] = _dot(lhs, jnp.concatenate([tu[u], jnp.concatenate([z128, vm[u]], axis=1)], axis=0))
    r_pair, y_pair, m_p, c_p, p_col = {}, {}, {}, {}, {}
    for (i, p) in pairs:
        u0, u1 = (i, p, 0), (i, p, 1)
        t2 = tu[u0] + tu[u1]
        r2 = ry[u0] + ry[u1]
        r_pair[i, p] = sl(i, p, "rt") + r2[:, 0:LANES]
        y_pair[i, p] = r2[:, LANES:]
        zl = jnp.concatenate([sl(i, p, "bh"), sl(i, p, "kh")], axis=0)
        zr = jnp.concatenate([t2, jnp.concatenate([z128, sl(i, p, "v")], axis=1)], axis=0)
        mc = _dot_tn(zl, zr)
        m_p[i, p] = jnp.where(blockdiag, mc[:, 0:LANES], 0.0)
        c_p[i, p] = jnp.where(blockdiag, mc[:, LANES:], 0.0)
        p_col[i, p] = jnp.sum(jnp.where(eye128, sl(i, p, "p_tot"), 0.0), axis=1, keepdims=True)
    hs = [h_ref[p] for p in range(4)]
    for i in range(nc):
        for p in range(4):
            sd = _dot(jnp.concatenate([r_pair[i, p], m_p[i, p]], axis=0), hs[p])
            y_ref[0, 0, rows_of[i], p * LANES:(p + 1) * LANES] = sd[0:L] + y_pair[i, p]
            hs[p] = p_col[i, p] * hs[p] + sd[L:] + c_p[i, p]
    for p in range(4):
        h_ref[p] = hs[p]


def _rwkv_scan(r, k, v, kk, lw, ic, k_a):
    B, T, _ = r.shape
    nc = SCAN_CHUNKS
    lb = nc * CHUNK
    ns = T // lb

    def blk(d, b, s):
        return jnp.where(d == 0, s, ns - 1 - s)

    shared = pl.BlockSpec((1, lb, 512), lambda d, b, s: (b, blk(d, b, s), 0))
    perdir = pl.BlockSpec((1, 1, lb, 512), lambda d, b, s: (d, b, blk(d, b, s), 0))
    return pl.pallas_call(
        functools.partial(_scan_kernel, nc=nc),
        grid=(2, B, ns),
        in_specs=[shared, shared, shared, shared, perdir, perdir,
                  pl.BlockSpec((1, 512), lambda d, b, s: (0, 0))],
        out_specs=perdir,
        out_shape=jax.ShapeDtypeStruct((2, B, T, 512), f32),
        scratch_shapes=[pltpu.VMEM((4, LANES, LANES), f32)],
        compiler_params=_cparams(("arbitrary", "arbitrary", "arbitrary")),
        name="rwkv_scan",
    )(r, k, v, kk, lw, ic, k_a)


def _attn_kernel(q_ref, k_ref, vt_ref, o_ref, m_ref, acc_ref, *scr):
    ki = pl.program_id(3)
    bk = k_ref.shape[1]

    @pl.when(ki == 0)
    def _():
        m_ref[...] = jnp.full_like(m_ref, -1e30)
        acc_ref[...] = jnp.zeros_like(acc_ref)

    bq = q_ref.shape[1]
    units = [(j, hh, c0) for c0 in range(0, bq, ATTN_QSUB) for j in range(2) for hh in range(2)]
    n = len(units)
    nb = len(scr) // 2
    s_refs = [scr[u % nb] for u in range(n)]
    p_refs = [scr[nb + u % nb] for u in range(n)]
    first = ki == 0
    m_old, m_new = [None] * n, [None] * n

    def scores(u):
        j, hh, c0 = units[u]
        s_refs[u][...] = lax.dot_general(k_ref[0, :, hh * LANES:(hh + 1) * LANES],
                                         q_ref[0, c0:c0 + ATTN_QSUB, j * LANES:(j + 1) * LANES],
                                         (((1,), (1,)), ((), ())), preferred_element_type=f32)

    def softmax(u):
        j, hh, c0 = units[u]
        m_old[u] = m_ref[2 * j + hh, :, c0:c0 + ATTN_QSUB]
        c = jnp.where(first, 0.0, m_old[u])
        smax = jnp.max(s_refs[u][...], axis=0, keepdims=True)
        t = jnp.maximum(jnp.where(first, -1e30, 0.0), smax - c).astype(bf16)
        m_new[u] = c + t.astype(f32)
        for r0 in range(0, bk, ATTN_STRIP):
            rs = slice(r0, r0 + ATTN_STRIP)
            p_refs[u][rs, :] = jnp.exp2((s_refs[u][rs, :] - c).astype(bf16) - t)

    def update(u):
        j, hh, c0 = units[u]
        i = 2 * j + hh
        pv = jnp.dot(vt_ref[0], p_refs[u][...], preferred_element_type=f32)
        acc_ref[i, :, c0:c0 + ATTN_QSUB] = (acc_ref[i, :, c0:c0 + ATTN_QSUB]
                                            * jnp.exp2(m_old[u] - m_new[u]) + pv)
        m_ref[i, :, c0:c0 + ATTN_QSUB] = m_new[u]

    scores(0)
    for u in range(n):
        if u + 1 < n:
            scores(u + 1)
        softmax(u)
        if u > 0:
            update(u - 1)
    update(n - 1)

    @pl.when(ki == pl.num_programs(3) - 1)
    def _():
        for j in range(2):
            o = jnp.concatenate([acc_ref[i, 0:HEAD_DIM, :] * (1.0 / acc_ref[i, HEAD_DIM:HEAD_DIM + 1, :])
                                 for i in (2 * j, 2 * j + 1)], axis=0)
            o_ref[0, :, j * LANES:(j + 1) * LANES] = o.T.astype(o_ref.dtype)


def _attention(q, k, vt):
    B, T, _ = q.shape
    bq, bk = ATTN_BQ, ATTN_BK
    nb = ATTN_BUFS
    return pl.pallas_call(
        _attn_kernel,
        grid=(B, KV_HEADS, T // bq, T // bk),
        in_specs=[pl.BlockSpec((1, bq, 256), lambda b, g, qi, ki: (b, qi, g)),
                  pl.BlockSpec((1, bk, 256), lambda b, g, qi, ki: (b, ki, g)),
                  pl.BlockSpec((1, VT_ROWS, bk), lambda b, g, qi, ki: (b, g, ki))],
        out_specs=pl.BlockSpec((1, bq, 256), lambda b, g, qi, ki: (b, qi, g)),
        out_shape=jax.ShapeDtypeStruct((B, T, ATTN_DIM), bf16),
        scratch_shapes=[pltpu.VMEM((4, 1, bq), f32), pltpu.VMEM((4, VT_ROWS, bq), f32)]
        + [pltpu.VMEM((bk, ATTN_QSUB), f32)] * nb + [pltpu.VMEM((bk, ATTN_QSUB), bf16)] * nb,
        compiler_params=_cparams(("parallel", "parallel", "parallel", "arbitrary")),
        name="gqa_attention",
    )(q, k, vt)


def _back_kernel(x_ref, y_ref, bonus_ref, xg_ref, attn_ref, mod_ref, gs_ref, gup_ref, wo_ref, lng_ref,
                 lnb_ref, ng_ref, wg_ref, wu_ref, wd_ref, o_ref):
    gs = gs_ref[...]
    y = y_ref[0, 0] + y_ref[1, 0]
    mu = _dot_exact_rhs(y, gs) * (1.0 / HEAD_DIM)
    yc = y - mu
    var = _dot_exact_rhs(yc * yc, gs) * (1.0 / HEAD_DIM)
    yn = yc * lax.rsqrt(var + GN_EPS) * lng_ref[...] + lnb_ref[...]
    g = _dot(_sigmoid(xg_ref[0]), gup_ref[...])
    rw = ((yn + bonus_ref[0]) * g).astype(bf16)
    mix = (jnp.dot(rw, wo_ref[0:RWKV_DIM, :], preferred_element_type=f32)
           + jnp.dot(attn_ref[0], wo_ref[RWKV_DIM:, :], preferred_element_type=f32))
    x = x_ref[0] + mod_ref[0, 2:3, :] * mix

    ms = jnp.mean(x * x, axis=-1, keepdims=True)
    h = x * lax.rsqrt(ms + NORM_EPS) * ng_ref[...]
    h = (h * (1.0 + mod_ref[0, 4:5, :]) + mod_ref[0, 3:4, :]).astype(bf16)
    gt = jnp.dot(h, wg_ref[...], preferred_element_type=f32)
    up = jnp.dot(h, wu_ref[...], preferred_element_type=f32)
    act = (gt * _sigmoid(gt) * up).astype(bf16)
    o_ref[0] = x + mod_ref[0, 5:6, :] * jnp.dot(act, wd_ref[...], preferred_element_type=f32)


def _back(x, y, bonus, xg, attn, mod, gsum, p):
    B, T, _ = x.shape
    bm = FFN_ROWS
    row3 = lambda b, i: (b, i, 0)
    const2 = lambda b, i: (0, 0)
    once = pl.Buffered(1)
    cst = lambda shape, imap=const2: pl.BlockSpec(shape, imap, pipeline_mode=once)
    return pl.pallas_call(
        _back_kernel,
        grid=(B, T // bm),
        in_specs=[pl.BlockSpec((1, bm, D_MODEL), row3),
                  pl.BlockSpec((2, 1, bm, 512), lambda b, i: (0, b, i, 0)),
                  pl.BlockSpec((1, bm, 512), row3),
                  pl.BlockSpec((1, bm, LANES), row3),
                  pl.BlockSpec((1, bm, 512), row3),
                  pl.BlockSpec((1, 6, D_MODEL), lambda b, i: (b, 0, 0)),
                  cst((512, 512)), cst((128, 512)), cst((D_MODEL, D_MODEL)), cst((1, 512)), cst((1, 512)),
                  cst((1, D_MODEL)),
                  cst((D_MODEL, D_FF)), cst((D_MODEL, D_FF), lambda b, i: (0, 1)), cst((D_FF, D_MODEL))],
        out_specs=pl.BlockSpec((1, bm, D_MODEL), row3),
        out_shape=jax.ShapeDtypeStruct((B, T, D_MODEL), f32),
        compiler_params=_cparams(("parallel", "parallel"), BACK_VMEM_LIMIT),
        name="back",
    )(x, y, bonus, xg, attn, mod, gsum, p["gate_up"], p["w_out"], p["ln_g"], p["ln_b"],
      p["norm_ffn_g"], p["w_ffn_in"], p["w_ffn_in"], p["w_ffn_out"])


def _rope_tables(T):
    pos = jnp.arange(T, dtype=jnp.int32)
    row = (pos // GRID_W).astype(f32)
    col = (pos % GRID_W).astype(f32)
    quarter = HEAD_DIM // 4
    freq = 1.0 / (ROPE_THETA ** (jnp.arange(quarter, dtype=f32) / quarter))
    ang_r = row[:, None] * freq[None, :]
    ang_c = col[:, None] * freq[None, :]
    cos = jnp.concatenate([jnp.cos(ang_r)] * 2 + [jnp.cos(ang_c)] * 2, axis=1)
    sin = jnp.concatenate([-jnp.sin(ang_r), jnp.sin(ang_r), -jnp.sin(ang_c), jnp.sin(ang_c)], axis=1)
    return jnp.concatenate([cos, cos], axis=1), jnp.concatenate([sin, sin], axis=1)


def _block_lora(up):
    z = jnp.zeros_like(up[0])
    return jnp.concatenate([jnp.concatenate([up[0], z], axis=1),
                            jnp.concatenate([z, up[1]], axis=1)], axis=0)


def _layer_params(l, ada_w, ada_b, norm_mix_g, norm_ffn_g, w_in, conv_w, decay_w0, decay_up, iclr_a0,
                  iclr_up, gate_up, k_k, k_a, r_k, ln_x_g, ln_x_b, q_norm_g, k_norm_g, w_out,
                  w_ffn_in, w_ffn_out):
    wi = w_in[l]
    wk = wi[:, 2048:2176]
    wv = wi[:, 2176:2304]
    wk2 = jnp.concatenate([wk[:, 0:64], wk[:, 0:64], wk[:, 64:128], wk[:, 64:128]], axis=1)
    w_all = jnp.concatenate([wi[:, 0:1536], wi[:, 2304:2688], wi[:, 1536:2048], wk2, wv], axis=1).astype(bf16)
    return dict(
        ada_w=ada_w[l], ada_b=ada_b[l],
        norm_mix_g=norm_mix_g[l].reshape(1, -1), norm_ffn_g=norm_ffn_g[l].reshape(1, -1),
        w_all=w_all, conv_w=conv_w[l],
        wd=_block_lora(decay_up[l]).astype(bf16), w0=decay_w0[l].reshape(1, -1),
        wi=_block_lora(iclr_up[l]).astype(bf16), a0=iclr_a0[l].reshape(1, -1),
        gate_up=gate_up[l].astype(bf16),
        k_k=k_k[l].reshape(1, -1), k_a=k_a[l].reshape(1, -1), r_k=r_k[l].reshape(1, -1),
        ln_g=ln_x_g[l].reshape(1, -1), ln_b=ln_x_b[l].reshape(1, -1),
        qg=jnp.tile(q_norm_g[l], 8).reshape(1, -1), kg=jnp.tile(k_norm_g[l], 4).reshape(1, -1),
        w_out=w_out[l].astype(bf16), w_ffn_in=w_ffn_in[l].astype(bf16),
        w_ffn_out=w_ffn_out[l].astype(bf16))


def _layer(x, c, p, cos, sin, gsum):
    mod = _modulation(c, p["ada_w"], p["ada_b"])
    r, kr, vr, kk, lw, ic, bonus, xg, q, k, vt = _front(x, mod, p, cos, sin, gsum)
    y = _rwkv_scan(r, kr, vr, kk, lw, ic, p["k_a"])
    attn = _attention(q, k, vt)
    return _back(x, y, bonus, xg, attn, mod, gsum, p)


def kernel(x_prompt, x_sample, c_prompt, c_sample, ada_w, ada_b, norm_mix_g, norm_ffn_g, w_in, conv_w,
           decay_w0, decay_up, iclr_a0, iclr_up, gate_up, k_k, k_a, r_k, ln_x_g, ln_x_b, q_norm_g,
           k_norm_g, w_out, w_ffn_in, w_ffn_out):
    depth = ada_w.shape[0]
    params = [_layer_params(l, ada_w, ada_b, norm_mix_g, norm_ffn_g, w_in, conv_w, decay_w0, decay_up,
                            iclr_a0, iclr_up, gate_up, k_k, k_a, r_k, ln_x_g, ln_x_b, q_norm_g,
                            k_norm_g, w_out, w_ffn_in, w_ffn_out) for l in range(depth)]
    head = np.arange(512) // HEAD_DIM
    gsum = jnp.asarray(head[:, None] == head[None, :], dtype=bf16)

    def run_trunk(x, c):
        cos, sin = _rope_tables(x.shape[1])
        for p in params:
            x = _layer(x, c, p, cos, sin, gsum)
        return x

    return (run_trunk(x_prompt, c_prompt), run_trunk(x_sample, c_sample))
```

```python
import functools
import math

import jax
import jax.numpy as jnp
import numpy as np
from jax import lax
from jax.experimental import pallas as pl
from jax.experimental.pallas import tpu as pltpu

f32 = jnp.float32
bf16 = jnp.bfloat16

D_MODEL = 1024
HEAD_DIM = 64
RWKV_DIM = 512
ATTN_DIM = 512
KV_HEADS = 2
LORA = 64
D_FF = 2816
GRID_W = 64
ROPE_THETA = 10000.0
NORM_EPS = 1e-6
QK_EPS = 1e-6
GN_EPS = 64e-5
DECAY_SCALE = math.exp(-0.5)

LANES = 128
VMEM_LIMIT = 48 * 1024 * 1024
BACK_VMEM_LIMIT = 56 * 1024 * 1024

ROW_BLOCK = 512
HALO = 16
FFN_ROWS = 512
CHUNK = 64
SCAN_CHUNKS = 4
ATTN_BQ = 2048
ATTN_BK = 2048
ATTN_BUFS = 3
ATTN_QSUB = 512
ATTN_STRIP = 16
LOG2E = 1.4426950408889634
VT_ROWS = 80

C_RKV = 0
C_LORA = 1536
C_Q = 1920
C_K = 2432
C_V = 2688
C_END = 2816


def _cparams(sem, vmem_limit=None):
    return pltpu.CompilerParams(dimension_semantics=sem, vmem_limit_bytes=vmem_limit or VMEM_LIMIT)


def _dot(a, b):
    return jnp.dot(a.astype(bf16), b.astype(bf16), preferred_element_type=f32)


def _dot_nt(a, b):
    return lax.dot_general(a.astype(bf16), b.astype(bf16), (((1,), (1,)), ((), ())),
                           preferred_element_type=f32)


def _dot_tn(a, b):
    return lax.dot_general(a.astype(bf16), b.astype(bf16), (((0,), (0,)), ((), ())),
                           preferred_element_type=f32)


def _split2(a):
    hi = a.astype(bf16)
    lo = (a - hi.astype(f32)).astype(bf16)
    return hi, lo


def _head_sum(a, g):
    hi, lo = _split2(a)
    w = g.shape[0]
    return jnp.concatenate([jnp.dot(hi[:, c:c + w], g, preferred_element_type=f32)
                            + jnp.dot(lo[:, c:c + w], g, preferred_element_type=f32)
                            for c in range(0, a.shape[1], w)], axis=1)


def _sigmoid(x):
    return 1.0 / (1.0 + jnp.exp(-x))


def _iota(shape, dim):
    return lax.broadcasted_iota(jnp.int32, shape, dim)


def _mod_kernel(c_ref, w_ref, b_ref, o_ref):
    c = c_ref[...]
    s = c * _sigmoid(c)
    sh, sl = _split2(s)
    wh, wl = _split2(w_ref[...])
    acc = jnp.dot(sh, wh, preferred_element_type=f32)
    acc += jnp.dot(sh, wl, preferred_element_type=f32)
    acc += jnp.dot(sl, wh, preferred_element_type=f32)
    o_ref[...] = acc + b_ref[...]


def _modulation(c, ada_w, ada_b):
    B = c.shape[0]
    Bp = max(8, B)
    cp = jnp.pad(c, ((0, Bp - B), (0, 0)))
    n = ada_w.shape[1] // D_MODEL
    out = pl.pallas_call(
        _mod_kernel,
        grid=(n,),
        in_specs=[pl.BlockSpec((Bp, D_MODEL), lambda j: (0, 0)),
                  pl.BlockSpec((D_MODEL, D_MODEL), lambda j: (0, j)),
                  pl.BlockSpec((1, D_MODEL), lambda j: (0, j))],
        out_specs=pl.BlockSpec((Bp, D_MODEL), lambda j: (0, j)),
        out_shape=jax.ShapeDtypeStruct((Bp, n * D_MODEL), f32),
        compiler_params=_cparams(("arbitrary",)),
        name="adaln_mod",
    )(cp, ada_w, ada_b.reshape(1, -1))
    return out[:B].reshape(B, n, D_MODEL)


def _rope(x, cos, sin):
    w = x.shape[1]
    up = pltpu.roll(x, w - 16, 1)
    dn = pltpu.roll(x, 16, 1)
    first = (_iota((1, w), 1) & 16) == 0
    return x * cos + jnp.where(first, up, dn) * sin


def _qk_norm(x, g_ref, gain):
    ms = _head_sum(x * x, g_ref[...]) * (1.0 / HEAD_DIM)
    return x * lax.rsqrt(ms + QK_EPS) * gain


def _front_kernel(x_ref, xp_ref, xn_ref, mod_ref, ng_ref, w_ref, cos_ref, sin_ref, qg_ref, kg_ref, gs_ref,
                  cw_ref, wd_ref, w0_ref, wi_ref, a0_ref, kk_ref, ka_ref, rk_ref,
                  r_ref, k_ref, v_ref, kkn_ref, lw_ref, ic_ref, bonus_ref, xg_ref, q_ref, ka_out_ref, vt_ref,
                  vs_ref):
    i = pl.program_id(1)
    n = pl.num_programs(1)
    bm = x_ref.shape[1]
    halo = xp_ref.shape[1]

    def ada_norm(x):
        ms = jnp.mean(x * x, axis=-1, keepdims=True)
        h = x * lax.rsqrt(ms + NORM_EPS) * ng_ref[...]
        return h * (1.0 + mod_ref[0, 1:2, :]) + mod_ref[0, 0:1, :]

    h_cur = ada_norm(x_ref[0])
    h_ext = jnp.concatenate([ada_norm(xp_ref[0]) * jnp.where(i > 0, 1.0, 0.0), h_cur,
                             ada_norm(xn_ref[0]) * jnp.where(i < n - 1, 1.0, 0.0)], axis=0)
    rkv_ext = jnp.dot(h_ext.astype(bf16), w_ref[:, C_RKV:C_LORA], preferred_element_type=f32)
    hb = h_cur.astype(bf16)
    lora = jnp.dot(hb, w_ref[:, C_LORA:C_Q], preferred_element_type=f32)
    q = jnp.dot(hb, w_ref[:, C_Q:C_K], preferred_element_type=f32)
    k = jnp.dot(hb, w_ref[:, C_K:C_V], preferred_element_type=f32)
    vs_ref[...] = jnp.dot(hb, w_ref[:, C_V:C_END], preferred_element_type=f32)

    vt = vs_ref[...].T.astype(bf16)
    pad = jnp.where(_iota((VT_ROWS - HEAD_DIM, vt.shape[1]), 0) == 0, 1.0, 0.0).astype(bf16)
    for g in range(KV_HEADS):
        vt_ref[0, g * VT_ROWS:g * VT_ROWS + HEAD_DIM, :] = vt[g * HEAD_DIM:(g + 1) * HEAD_DIM]
        vt_ref[0, g * VT_ROWS + HEAD_DIM:(g + 1) * VT_ROWS, :] = pad
    cos = cos_ref[...]
    sin = sin_ref[...]
    cos4 = jnp.concatenate([cos] * 4, axis=1)
    sin4 = jnp.concatenate([sin] * 4, axis=1)
    qn = _rope(_qk_norm(q, gs_ref, qg_ref[...]), cos4, sin4)
    q_ref[0] = (qn * (LOG2E * HEAD_DIM ** -0.5)).astype(bf16)
    kn = _rope(_qk_norm(k, gs_ref, kg_ref[...]), cos4[:, 0:256], sin4[:, 0:256])
    half0 = _iota((1, LANES), 1) < HEAD_DIM
    ka_out_ref[0] = jnp.concatenate([jnp.where(keep, kn[:, g * LANES:(g + 1) * LANES], 0.0)
                                     for g in range(KV_HEADS) for keep in (half0, jnp.logical_not(half0))],
                                    axis=1).astype(bf16)

    next_shift = bm + 2 * halo - 1
    outs = (r_ref, k_ref, v_ref)
    rkv = []
    for p in range(3):
        cols = slice(p * 512, (p + 1) * 512)
        ext = rkv_ext[:, cols]
        y = (cw_ref[0:1, cols] * pltpu.roll(ext, 1, 0)[halo:halo + bm]
             + cw_ref[1:2, cols] * ext[halo:halo + bm]
             + cw_ref[2:3, cols] * pltpu.roll(ext, next_shift, 0)[halo:halo + bm])
        outs[p][0] = y
        rkv.append(y)
    r, kr, vr = rkv
    kkh = kr * kk_ref[...]
    kkn_ref[0] = kkh * lax.rsqrt(_head_sum(kkh * kkh, gs_ref[...]) + 1e-12)
    dl = _dot(jnp.tanh(lora[:, 0:128]), wd_ref[...]) + w0_ref[...]
    lw = -DECAY_SCALE * _sigmoid(dl)
    lw_ref[0, 0] = lw[:, 0:512]
    lw_ref[1, 0] = lw[:, 512:1024]
    ic = _sigmoid(_dot(lora[:, 128:256], wi_ref[...]) + a0_ref[...])
    ic_ref[0, 0] = ic[:, 0:512]
    ic_ref[1, 0] = ic[:, 512:1024]
    xg_ref[0] = lora[:, 256:384]
    kb = kr * (1.0 + (0.5 * (ic[:, 0:512] + ic[:, 512:1024]) - 1.0) * ka_ref[...])
    bonus_ref[0] = _head_sum(r * kb * rk_ref[...], gs_ref[...]) * vr


def _front(x, mod, p, cos, sin, gsum):
    B, T, _ = x.shape
    bm = ROW_BLOCK
    hb = bm // HALO
    nh = T // HALO
    row3 = lambda b, i: (b, i, 0)
    const2 = lambda b, i: (0, 0)
    once = pl.Buffered(1)
    cst = lambda shape: pl.BlockSpec(shape, const2, pipeline_mode=once)
    o512 = jax.ShapeDtypeStruct((B, T, 512), f32)
    o2 = jax.ShapeDtypeStruct((2, B, T, 512), f32)
    dir4 = pl.BlockSpec((2, 1, bm, 512), lambda b, i: (0, b, i, 0))
    return pl.pallas_call(
        _front_kernel,
        grid=(B, T // bm),
        in_specs=[pl.BlockSpec((1, bm, D_MODEL), row3),
                  pl.BlockSpec((1, HALO, D_MODEL), lambda b, i: (b, jnp.maximum(i * hb - 1, 0), 0)),
                  pl.BlockSpec((1, HALO, D_MODEL), lambda b, i: (b, jnp.minimum((i + 1) * hb, nh - 1), 0)),
                  pl.BlockSpec((1, 6, D_MODEL), lambda b, i: (b, 0, 0)),
                  cst((1, D_MODEL)), cst((D_MODEL, C_END)),
                  pl.BlockSpec((bm, LANES), lambda b, i: (i, 0)),
                  pl.BlockSpec((bm, LANES), lambda b, i: (i, 0)),
                  cst((1, 512)), cst((1, 256)), cst((256, 256)),
                  cst((3, 1536)), cst((128, 1024)), cst((1, 1024)), cst((128, 1024)), cst((1, 1024)),
                  cst((1, 512)), cst((1, 512)), cst((1, 512))],
        out_specs=[pl.BlockSpec((1, bm, 512), row3)] * 4 + [dir4, dir4]
        + [pl.BlockSpec((1, bm, 512), row3), pl.BlockSpec((1, bm, LANES), row3),
           pl.BlockSpec((1, bm, 512), row3), pl.BlockSpec((1, bm, 512), row3),
           pl.BlockSpec((1, KV_HEADS * VT_ROWS, bm), lambda b, i: (b, 0, i))],
        out_shape=[o512, o512, o512, o512, o2, o2, o512,
                   jax.ShapeDtypeStruct((B, T, LANES), f32),
                   jax.ShapeDtypeStruct((B, T, 512), bf16),
                   jax.ShapeDtypeStruct((B, T, 512), bf16),
                   jax.ShapeDtypeStruct((B, KV_HEADS * VT_ROWS, T), bf16)],
        scratch_shapes=[pltpu.VMEM((bm, KV_HEADS * HEAD_DIM), f32)],
        compiler_params=_cparams(("parallel", "parallel")),
        name="front",
    )(x, x, x, mod, p["norm_mix_g"], p["w_all"], cos, sin, p["qg"], p["kg"], gsum,
      p["conv_w"], p["wd"], p["w0"], p["wi"], p["a0"], p["k_k"], p["k_a"], p["r_k"])


def _scan_kernel(r_ref, k_ref, v_ref, kk_ref, lw_ref, ic_ref, ka_ref, y_ref, h_ref, *, nc):
    L = CHUNK
    d = pl.program_id(0)
    fwd = d == 0

    @pl.when(pl.program_id(2) == 0)
    def _():
        h_ref[...] = jnp.zeros_like(h_ref)

    sgn = jnp.where(fwd, 1, -1)
    col = _iota((L, LANES), 1)
    row = _iota((L, LANES), 0)
    order = ((col & (L - 1)) - row) * sgn
    left = col < L
    strict_l = (order < 0) & left
    strict_r = (order < 0) & jnp.logical_not(left)
    incl = order <= 0
    eye_r = jnp.where(col - L == row, 1.0, 0.0)
    right_f = jnp.where(left, 0.0, 1.0)
    incl_bf = jnp.where(order[:, 0:L] <= 0, 1.0, 0.0).astype(bf16)
    lane = _iota((1, LANES), 1)
    head_m = (jnp.where(lane < L, 1.0, 0.0), jnp.where(lane < L, 0.0, 1.0))
    blockdiag = (_iota((LANES, LANES), 0) < L) == (_iota((LANES, LANES), 1) < L)
    eye128 = _iota((LANES, LANES), 0) == _iota((LANES, LANES), 1)
    z128 = jnp.zeros((L, LANES), f32)
    z256 = jnp.zeros((L, 2 * LANES), f32)
    ka = ka_ref[...]
    pairs = [(i, p) for i in range(nc) for p in range(4)]
    units = [(i, p, hh) for (i, p) in pairs for hh in range(2)]

    rows_of, ch = [], []
    for i in range(nc):
        ci = jnp.where(fwd, i, nc - 1 - i)
        rows = pl.ds(pl.multiple_of(ci * L, L), L)
        rows_of.append(rows)
        lw = lw_ref[0, 0, rows, :]
        ic = ic_ref[0, 0, rows, :]
        kk = kk_ref[0, rows, :]
        kd = k_ref[0, rows, :] * (1.0 + (ic - 1.0) * ka)
        b = kk * ic
        l1 = lw.astype(bf16)
        e1 = lw - l1.astype(f32)
        l2 = e1.astype(bf16)
        l3 = (e1 - l2.astype(f32)).astype(bf16)
        cum = (jnp.dot(incl_bf, l1, preferred_element_type=f32)
               + jnp.dot(incl_bf, l2, preferred_element_type=f32)
               + jnp.dot(incl_bf, l3, preferred_element_type=f32))
        tot = jnp.where(fwd, cum[L - 1:L, :], cum[0:1, :])
        p_inv = jnp.exp(-cum)
        p_end = jnp.exp(tot - cum)
        ch.append(dict(
            v=v_ref[0, rows, :], p_tot=jnp.exp(tot),
            rt=r_ref[0, rows, :] * jnp.exp(cum), at=-kk * jnp.exp(cum - lw),
            bt=b * p_inv, kt=kd * p_inv, bh=b * p_end, kh=kd * p_end))

    def sl(i, p, name):
        return ch[i][name][:, p * LANES:(p + 1) * LANES]

    am, vm, g = {}, {}, {}
    for (i, p) in pairs:
        z = jnp.concatenate([sl(i, p, "bt"), sl(i, p, "kt")], axis=0).astype(bf16)
        for hh in range(2):
            u = (i, p, hh)
            am[u] = sl(i, p, "at") * head_m[hh]
            vm[u] = sl(i, p, "v") * head_m[hh]
            xm = jnp.concatenate([am[u], sl(i, p, "rt") * head_m[hh]], axis=0)
            g[u] = _dot_nt(xm, z)
    pk, w1 = {}, {}
    for u in units:
        gt = g[u][0:L]
        pk[u] = jnp.where(strict_l, gt, 0.0) + eye_r
        w1[u] = _dot(jnp.where(strict_r, gt, 0.0), jnp.concatenate([z128, vm[u]], axis=0))
    for _ in range(6):
        for u in units:
            prod = _dot(pk[u], jnp.concatenate([pk[u], z128], axis=0))
            pk[u] = prod + pk[u] * right_f
    tu, ry = {}, {}
    for u in units:
        tu[u] = _dot(pk[u], jnp.concatenate([z256, jnp.concatenate([am[u], w1[u]], axis=1)], axis=0))
    for u in units:
        lhs = jnp.where(incl, g[u][L:2 * L], 0.0)
        ry[u] = _dot(lhs, jnp.concatenate([tu[u], jnp.concatenate([z128, vm[u]], axis=1)], axis=0))
    r_pair, y_pair, m_p, c_p, p_col = {}, {}, {}, {}, {}
    for (i, p) in pairs:
        u0, u1 = (i, p, 0), (i, p, 1)
        t2 = tu[u0] + tu[u1]
        r2 = ry[u0] + ry[u1]
        r_pair[i, p] = sl(i, p, "rt") + r2[:, 0:LANES]
        y_pair[i, p] = r2[:, LANES:]
        zl = jnp.concatenate([sl(i, p, "bh"), sl(i, p, "kh")], axis=0)
        zr = jnp.concatenate([t2, jnp.concatenate([z128, sl(i, p, "v")], axis=1)], axis=0)
        mc = _dot_tn(zl, zr)
        m_p[i, p] = jnp.where(blockdiag, mc[:, 0:LANES], 0.0)
        c_p[i, p] = jnp.where(blockdiag, mc[:, LANES:], 0.0)
        p_col[i, p] = jnp.sum(jnp.where(eye128, sl(i, p, "p_tot"), 0.0), axis=1, keepdims=True)
    hs = [h_ref[p] for p in range(4)]
    for i in range(nc):
        for p in range(4):
            sd = _dot(jnp.concatenate([r_pair[i, p], m_p[i, p]], axis=0), hs[p])
            y_ref[0, 0, rows_of[i], p * LANES:(p + 1) * LANES] = sd[0:L] + y_pair[i, p]
            hs[p] = p_col[i, p] * hs[p] + sd[L:] + c_p[i, p]
    for p in range(4):
        h_ref[p] = hs[p]


def _rwkv_scan(r, k, v, kk, lw, ic, k_a):
    B, T, _ = r.shape
    nc = SCAN_CHUNKS
    lb = nc * CHUNK
    ns = T // lb

    def blk(d, b, s):
        return jnp.where(d == 0, s, ns - 1 - s)

    shared = pl.BlockSpec((1, lb, 512), lambda d, b, s: (b, blk(d, b, s), 0))
    perdir = pl.BlockSpec((1, 1, lb, 512), lambda d, b, s: (d, b, blk(d, b, s), 0))
    return pl.pallas_call(
        functools.partial(_scan_kernel, nc=nc),
        grid=(2, B, ns),
        in_specs=[shared, shared, shared, shared, perdir, perdir,
                  pl.BlockSpec((1, 512), lambda d, b, s: (0, 0))],
        out_specs=perdir,
        out_shape=jax.ShapeDtypeStruct((2, B, T, 512), f32),
        scratch_shapes=[pltpu.VMEM((4, LANES, LANES), f32)],
        compiler_params=_cparams(("arbitrary", "arbitrary", "arbitrary")),
        name="rwkv_scan",
    )(r, k, v, kk, lw, ic, k_a)


def _attn_kernel(q_ref, k_ref, vt_ref, o_ref, m_ref, acc_ref, *scr):
    ki = pl.program_id(3)
    bk = k_ref.shape[1]

    @pl.when(ki == 0)
    def _():
        m_ref[...] = jnp.full_like(m_ref, -1e30)
        acc_ref[...] = jnp.zeros_like(acc_ref)

    bq = q_ref.shape[1]
    units = [(j, hh, c0) for c0 in range(0, bq, ATTN_QSUB) for j in range(2) for hh in range(2)]
    n = len(units)
    nb = len(scr) // 2
    s_refs = [scr[u % nb] for u in range(n)]
    p_refs = [scr[nb + u % nb] for u in range(n)]
    first = ki == 0
    m_old, m_new = [None] * n, [None] * n

    def scores(u):
        j, hh, c0 = units[u]
        s_refs[u][...] = lax.dot_general(k_ref[0, :, hh * LANES:(hh + 1) * LANES],
                                         q_ref[0, c0:c0 + ATTN_QSUB, j * LANES:(j + 1) * LANES],
                                         (((1,), (1,)), ((), ())), preferred_element_type=f32)

    def softmax(u):
        j, hh, c0 = units[u]
        m_old[u] = m_ref[2 * j + hh, :, c0:c0 + ATTN_QSUB]
        c = jnp.where(first, 0.0, m_old[u])
        smax = jnp.max(s_refs[u][...], axis=0, keepdims=True)
        t = jnp.maximum(jnp.where(first, -1e30, 0.0), smax - c).astype(bf16)
        m_new[u] = c + t.astype(f32)
        for r0 in range(0, bk, ATTN_STRIP):
            rs = slice(r0, r0 + ATTN_STRIP)
            p_refs[u][rs, :] = jnp.exp2((s_refs[u][rs, :] - c).astype(bf16) - t)

    def update(u):
        j, hh, c0 = units[u]
        i = 2 * j + hh
        pv = jnp.dot(vt_ref[0], p_refs[u][...], preferred_element_type=f32)
        acc_ref[i, :, c0:c0 + ATTN_QSUB] = (acc_ref[i, :, c0:c0 + ATTN_QSUB]
                                            * jnp.exp2(m_old[u] - m_new[u]) + pv)
        m_ref[i, :, c0:c0 + ATTN_QSUB] = m_new[u]

    scores(0)
    for u in range(n):
        if u + 1 < n:
            scores(u + 1)
        softmax(u)
        if u > 0:
            update(u - 1)
    update(n - 1)

    @pl.when(ki == pl.num_programs(3) - 1)
    def _():
        for j in range(2):
            o = jnp.concatenate([acc_ref[i, 0:HEAD_DIM, :] * (1.0 / acc_ref[i, HEAD_DIM:HEAD_DIM + 1, :])
                                 for i in (2 * j, 2 * j + 1)], axis=0)
            o_ref[0, :, j * LANES:(j + 1) * LANES] = o.T.astype(o_ref.dtype)


def _attention(q, k, vt):
    B, T, _ = q.shape
    bq, bk = ATTN_BQ, ATTN_BK
    nb = ATTN_BUFS
    return pl.pallas_call(
        _attn_kernel,
        grid=(B, KV_HEADS, T // bq, T // bk),
        in_specs=[pl.BlockSpec((1, bq, 256), lambda b, g, qi, ki: (b, qi, g)),
                  pl.BlockSpec((1, bk, 256), lambda b, g, qi, ki: (b, ki, g)),
                  pl.BlockSpec((1, VT_ROWS, bk), lambda b, g, qi, ki: (b, g, ki))],
        out_specs=pl.BlockSpec((1, bq, 256), lambda b, g, qi, ki: (b, qi, g)),
        out_shape=jax.ShapeDtypeStruct((B, T, ATTN_DIM), bf16),
        scratch_shapes=[pltpu.VMEM((4, 1, bq), f32), pltpu.VMEM((4, VT_ROWS, bq), f32)]
        + [pltpu.VMEM((bk, ATTN_QSUB), f32)] * nb + [pltpu.VMEM((bk, ATTN_QSUB), bf16)] * nb,
        compiler_params=_cparams(("parallel", "parallel", "parallel", "arbitrary")),
        name="gqa_attention",
    )(q, k, vt)


def _back_kernel(x_ref, y_ref, bonus_ref, xg_ref, attn_ref, mod_ref, gs_ref, gup_ref, wo_ref, lng_ref,
                 lnb_ref, ng_ref, wg_ref, wu_ref, wd_ref, o_ref):
    gs = gs_ref[...]
    y = y_ref[0, 0] + y_ref[1, 0]
    mu = _head_sum(y, gs) * (1.0 / HEAD_DIM)
    yc = y - mu
    var = _head_sum(yc * yc, gs) * (1.0 / HEAD_DIM)
    yn = yc * lax.rsqrt(var + GN_EPS) * lng_ref[...] + lnb_ref[...]
    g = _dot(_sigmoid(xg_ref[0]), gup_ref[...])
    rw = ((yn + bonus_ref[0]) * g).astype(bf16)
    mix = (jnp.dot(rw, wo_ref[0:RWKV_DIM, :], preferred_element_type=f32)
           + jnp.dot(attn_ref[0], wo_ref[RWKV_DIM:, :], preferred_element_type=f32))
    x = x_ref[0] + mod_ref[0, 2:3, :] * mix

    ms = jnp.mean(x * x, axis=-1, keepdims=True)
    h = x * lax.rsqrt(ms + NORM_EPS) * ng_ref[...]
    h = (h * (1.0 + mod_ref[0, 4:5, :]) + mod_ref[0, 3:4, :]).astype(bf16)
    gt = jnp.dot(h, wg_ref[...], preferred_element_type=f32)
    up = jnp.dot(h, wu_ref[...], preferred_element_type=f32)
    act = (gt * _sigmoid(gt) * up).astype(bf16)
    o_ref[0] = x + mod_ref[0, 5:6, :] * jnp.dot(act, wd_ref[...], preferred_element_type=f32)


def _back(x, y, bonus, xg, attn, mod, gsum, p):
    B, T, _ = x.shape
    bm = FFN_ROWS
    row3 = lambda b, i: (b, i, 0)
    const2 = lambda b, i: (0, 0)
    once = pl.Buffered(1)
    cst = lambda shape, imap=const2: pl.BlockSpec(shape, imap, pipeline_mode=once)
    return pl.pallas_call(
        _back_kernel,
        grid=(B, T // bm),
        in_specs=[pl.BlockSpec((1, bm, D_MODEL), row3),
                  pl.BlockSpec((2, 1, bm, 512), lambda b, i: (0, b, i, 0)),
                  pl.BlockSpec((1, bm, 512), row3),
                  pl.BlockSpec((1, bm, LANES), row3),
                  pl.BlockSpec((1, bm, 512), row3),
                  pl.BlockSpec((1, 6, D_MODEL), lambda b, i: (b, 0, 0)),
                  cst((256, 256)), cst((128, 512)), cst((D_MODEL, D_MODEL)), cst((1, 512)), cst((1, 512)),
                  cst((1, D_MODEL)),
                  cst((D_MODEL, D_FF)), cst((D_MODEL, D_FF), lambda b, i: (0, 1)), cst((D_FF, D_MODEL))],
        out_specs=pl.BlockSpec((1, bm, D_MODEL), row3),
        out_shape=jax.ShapeDtypeStruct((B, T, D_MODEL), f32),
        compiler_params=_cparams(("parallel", "parallel"), BACK_VMEM_LIMIT),
        name="back",
    )(x, y, bonus, xg, attn, mod, gsum, p["gate_up"], p["w_out"], p["ln_g"], p["ln_b"],
      p["norm_ffn_g"], p["w_ffn_in"], p["w_ffn_in"], p["w_ffn_out"])


def _rope_tables(T):
    rows = T // GRID_W
    quarter = HEAD_DIM // 4
    freq = 1.0 / (ROPE_THETA ** (np.arange(quarter, dtype=np.float64) / quarter))
    ang_r = np.arange(rows, dtype=np.float64)[:, None] * freq[None, :]
    ang_c = np.arange(GRID_W, dtype=np.float64)[:, None] * freq[None, :]

    def table(fr, fc):
        r = jnp.broadcast_to(jnp.asarray(fr, f32)[:, None, :], (rows, GRID_W, 2 * quarter))
        c = jnp.broadcast_to(jnp.asarray(fc, f32)[None, :, :], (rows, GRID_W, 2 * quarter))
        t = jnp.concatenate([r, c], axis=-1).reshape(T, HEAD_DIM)
        return jnp.concatenate([t, t], axis=1)

    cos = table(np.concatenate([np.cos(ang_r)] * 2, axis=1), np.concatenate([np.cos(ang_c)] * 2, axis=1))
    sin = table(np.concatenate([-np.sin(ang_r), np.sin(ang_r)], axis=1),
                np.concatenate([-np.sin(ang_c), np.sin(ang_c)], axis=1))
    return cos, sin


def _block_lora(up):
    z = jnp.zeros_like(up[0])
    return jnp.concatenate([jnp.concatenate([up[0], z], axis=1),
                            jnp.concatenate([z, up[1]], axis=1)], axis=0)


def _layer_params(l, ada_w, ada_b, norm_mix_g, norm_ffn_g, w_in, conv_w, decay_w0, decay_up, iclr_a0,
                  iclr_up, gate_up, k_k, k_a, r_k, ln_x_g, ln_x_b, q_norm_g, k_norm_g, w_out,
                  w_ffn_in, w_ffn_out):
    wi = w_in[l].astype(bf16)
    wk = wi[:, 2048:2176]
    wv = wi[:, 2176:2304]
    wk2 = jnp.concatenate([wk[:, 0:64], wk[:, 0:64], wk[:, 64:128], wk[:, 64:128]], axis=1)
    w_all = jnp.concatenate([wi[:, 0:1536], wi[:, 2304:2688], wi[:, 1536:2048], wk2, wv], axis=1)
    return dict(
        ada_w=ada_w[l], ada_b=ada_b[l],
        norm_mix_g=norm_mix_g[l].reshape(1, -1), norm_ffn_g=norm_ffn_g[l].reshape(1, -1),
        w_all=w_all, conv_w=conv_w[l],
        wd=_block_lora(decay_up[l]).astype(bf16), w0=decay_w0[l].reshape(1, -1),
        wi=_block_lora(iclr_up[l]).astype(bf16), a0=iclr_a0[l].reshape(1, -1),
        gate_up=gate_up[l].astype(bf16),
        k_k=k_k[l].reshape(1, -1), k_a=k_a[l].reshape(1, -1), r_k=r_k[l].reshape(1, -1),
        ln_g=ln_x_g[l].reshape(1, -1), ln_b=ln_x_b[l].reshape(1, -1),
        qg=jnp.tile(q_norm_g[l], 8).reshape(1, -1), kg=jnp.tile(k_norm_g[l], 4).reshape(1, -1),
        w_out=w_out[l].astype(bf16), w_ffn_in=w_ffn_in[l].astype(bf16),
        w_ffn_out=w_ffn_out[l].astype(bf16))


def _layer(x, c, p, cos, sin, gsum):
    mod = _modulation(c, p["ada_w"], p["ada_b"])
    r, kr, vr, kk, lw, ic, bonus, xg, q, k, vt = _front(x, mod, p, cos, sin, gsum)
    y = _rwkv_scan(r, kr, vr, kk, lw, ic, p["k_a"])
    attn = _attention(q, k, vt)
    return _back(x, y, bonus, xg, attn, mod, gsum, p)


def kernel(x_prompt, x_sample, c_prompt, c_sample, ada_w, ada_b, norm_mix_g, norm_ffn_g, w_in, conv_w,
           decay_w0, decay_up, iclr_a0, iclr_up, gate_up, k_k, k_a, r_k, ln_x_g, ln_x_b, q_norm_g,
           k_norm_g, w_out, w_ffn_in, w_ffn_out):
    depth = ada_w.shape[0]
    params = [_layer_params(l, ada_w, ada_b, norm_mix_g, norm_ffn_g, w_in, conv_w, decay_w0, decay_up,
                            iclr_a0, iclr_up, gate_up, k_k, k_a, r_k, ln_x_g, ln_x_b, q_norm_g,
                            k_norm_g, w_out, w_ffn_in, w_ffn_out) for l in range(depth)]
    head = np.arange(256) // HEAD_DIM
    gsum = jnp.asarray(head[:, None] == head[None, :], dtype=bf16)

    def run_trunk(x, c):
        cos, sin = _rope_tables(x.shape[1])
        for p in params:
            x = _layer(x, c, p, cos, sin, gsum)
        return x

    return (run_trunk(x_prompt, c_prompt), run_trunk(x_sample, c_sample))
```

```python
import functools
import math

import jax
import jax.numpy as jnp
import numpy as np
from jax import lax
from jax.experimental import pallas as pl
from jax.experimental.pallas import tpu as pltpu

f32 = jnp.float32
bf16 = jnp.bfloat16

D_MODEL = 1024
HEAD_DIM = 64
RWKV_DIM = 512
ATTN_DIM = 512
KV_HEADS = 2
LORA = 64
D_FF = 2816
GRID_W = 64
ROPE_THETA = 10000.0
NORM_EPS = 1e-6
QK_EPS = 1e-6
GN_EPS = 64e-5
DECAY_SCALE = math.exp(-0.5)

LANES = 128
VMEM_LIMIT = 48 * 1024 * 1024
BACK_VMEM_LIMIT = 56 * 1024 * 1024

ROW_BLOCK = 512
HALO = 16
FFN_ROWS = 512
CHUNK = 64
SCAN_CHUNKS = 4
ATTN_BQ = 2048
ATTN_BK = 2048
ATTN_BUFS = 3
ATTN_QSUB = 256
ATTN_STRIP = 16
LOG2E = 1.4426950408889634
VT_ROWS = 80

C_RKV = 0
C_LORA = 1536
C_Q = 1920
C_K = 2432
C_V = 2688
C_END = 2816


def _cparams(sem, vmem_limit=None):
    return pltpu.CompilerParams(dimension_semantics=sem, vmem_limit_bytes=vmem_limit or VMEM_LIMIT)


def _dot(a, b):
    return jnp.dot(a.astype(bf16), b.astype(bf16), preferred_element_type=f32)


def _dot_nt(a, b):
    return lax.dot_general(a.astype(bf16), b.astype(bf16), (((1,), (1,)), ((), ())),
                           preferred_element_type=f32)


def _dot_tn(a, b):
    return lax.dot_general(a.astype(bf16), b.astype(bf16), (((0,), (0,)), ((), ())),
                           preferred_element_type=f32)


def _split2(a):
    hi = a.astype(bf16)
    lo = (a - hi.astype(f32)).astype(bf16)
    return hi, lo


def _head_sum(a, g):
    hi, lo = _split2(a)
    w = g.shape[0]
    return jnp.concatenate([jnp.dot(hi[:, c:c + w], g, preferred_element_type=f32)
                            + jnp.dot(lo[:, c:c + w], g, preferred_element_type=f32)
                            for c in range(0, a.shape[1], w)], axis=1)


def _sigmoid(x):
    return 1.0 / (1.0 + jnp.exp(-x))


def _iota(shape, dim):
    return lax.broadcasted_iota(jnp.int32, shape, dim)


def _mod_kernel(c_ref, w_ref, b_ref, o_ref):
    c = c_ref[...]
    s = c * _sigmoid(c)
    sh, sl = _split2(s)
    wh, wl = _split2(w_ref[...])
    acc = jnp.dot(sh, wh, preferred_element_type=f32)
    acc += jnp.dot(sh, wl, preferred_element_type=f32)
    acc += jnp.dot(sl, wh, preferred_element_type=f32)
    o_ref[...] = acc + b_ref[...]


def _modulation(c, ada_w, ada_b):
    B = c.shape[0]
    Bp = max(8, B)
    cp = jnp.pad(c, ((0, Bp - B), (0, 0)))
    n = ada_w.shape[1] // D_MODEL
    out = pl.pallas_call(
        _mod_kernel,
        grid=(n,),
        in_specs=[pl.BlockSpec((Bp, D_MODEL), lambda j: (0, 0)),
                  pl.BlockSpec((D_MODEL, D_MODEL), lambda j: (0, j)),
                  pl.BlockSpec((1, D_MODEL), lambda j: (0, j))],
        out_specs=pl.BlockSpec((Bp, D_MODEL), lambda j: (0, j)),
        out_shape=jax.ShapeDtypeStruct((Bp, n * D_MODEL), f32),
        compiler_params=_cparams(("arbitrary",)),
        name="adaln_mod",
    )(cp, ada_w, ada_b.reshape(1, -1))
    return out[:B].reshape(B, n, D_MODEL)


def _rope(x, cos, sin):
    w = x.shape[1]
    up = pltpu.roll(x, w - 16, 1)
    dn = pltpu.roll(x, 16, 1)
    first = (_iota((1, w), 1) & 16) == 0
    return x * cos + jnp.where(first, up, dn) * sin


def _qk_norm(x, g_ref, gain):
    ms = _head_sum(x * x, g_ref[...]) * (1.0 / HEAD_DIM)
    return x * lax.rsqrt(ms + QK_EPS) * gain


def _front_kernel(x_ref, xp_ref, xn_ref, mod_ref, ng_ref, w_ref, cos_ref, sin_ref, qg_ref, kg_ref, gs_ref,
                  cw_ref, wd_ref, w0_ref, wi_ref, a0_ref, kk_ref, ka_ref, rk_ref,
                  r_ref, k_ref, v_ref, kkn_ref, lw_ref, ic_ref, bonus_ref, xg_ref, q_ref, ka_out_ref, vt_ref,
                  vs_ref):
    i = pl.program_id(1)
    n = pl.num_programs(1)
    bm = x_ref.shape[1]
    halo = xp_ref.shape[1]

    def ada_norm(x):
        ms = jnp.mean(x * x, axis=-1, keepdims=True)
        h = x * lax.rsqrt(ms + NORM_EPS) * ng_ref[...]
        return h * (1.0 + mod_ref[0, 1:2, :]) + mod_ref[0, 0:1, :]

    h_cur = ada_norm(x_ref[0])
    h_ext = jnp.concatenate([ada_norm(xp_ref[0]) * jnp.where(i > 0, 1.0, 0.0), h_cur,
                             ada_norm(xn_ref[0]) * jnp.where(i < n - 1, 1.0, 0.0)], axis=0)
    rkv_ext = jnp.dot(h_ext.astype(bf16), w_ref[:, C_RKV:C_LORA], preferred_element_type=f32)
    hb = h_cur.astype(bf16)
    lora = jnp.dot(hb, w_ref[:, C_LORA:C_Q], preferred_element_type=f32)
    q = jnp.dot(hb, w_ref[:, C_Q:C_K], preferred_element_type=f32)
    k = jnp.dot(hb, w_ref[:, C_K:C_V], preferred_element_type=f32)
    vs_ref[...] = jnp.dot(hb, w_ref[:, C_V:C_END], preferred_element_type=f32)

    vt = vs_ref[...].T.astype(bf16)
    pad = jnp.where(_iota((VT_ROWS - HEAD_DIM, vt.shape[1]), 0) == 0, 1.0, 0.0).astype(bf16)
    for g in range(KV_HEADS):
        vt_ref[0, g * VT_ROWS:g * VT_ROWS + HEAD_DIM, :] = vt[g * HEAD_DIM:(g + 1) * HEAD_DIM]
        vt_ref[0, g * VT_ROWS + HEAD_DIM:(g + 1) * VT_ROWS, :] = pad
    cos = cos_ref[...]
    sin = sin_ref[...]
    cos4 = jnp.concatenate([cos] * 4, axis=1)
    sin4 = jnp.concatenate([sin] * 4, axis=1)
    qn = _rope(_qk_norm(q, gs_ref, qg_ref[...]), cos4, sin4)
    q_ref[0] = (qn * (LOG2E * HEAD_DIM ** -0.5)).astype(bf16)
    kn = _rope(_qk_norm(k, gs_ref, kg_ref[...]), cos4[:, 0:256], sin4[:, 0:256])
    half0 = _iota((1, LANES), 1) < HEAD_DIM
    ka_out_ref[0] = jnp.concatenate([jnp.where(keep, kn[:, g * LANES:(g + 1) * LANES], 0.0)
                                     for g in range(KV_HEADS) for keep in (half0, jnp.logical_not(half0))],
                                    axis=1).astype(bf16)

    next_shift = bm + 2 * halo - 1
    outs = (r_ref, k_ref, v_ref)
    rkv = []
    for p in range(3):
        cols = slice(p * 512, (p + 1) * 512)
        ext = rkv_ext[:, cols]
        y = (cw_ref[0:1, cols] * pltpu.roll(ext, 1, 0)[halo:halo + bm]
             + cw_ref[1:2, cols] * ext[halo:halo + bm]
             + cw_ref[2:3, cols] * pltpu.roll(ext, next_shift, 0)[halo:halo + bm])
        outs[p][0] = y
        rkv.append(y)
    r, kr, vr = rkv
    kkh = kr * kk_ref[...]
    kkn_ref[0] = kkh * lax.rsqrt(_head_sum(kkh * kkh, gs_ref[...]) + 1e-12)
    dl = _dot(jnp.tanh(lora[:, 0:128]), wd_ref[...]) + w0_ref[...]
    lw = -DECAY_SCALE * _sigmoid(dl)
    lw_ref[0, 0] = lw[:, 0:512]
    lw_ref[1, 0] = lw[:, 512:1024]
    ic = _sigmoid(_dot(lora[:, 128:256], wi_ref[...]) + a0_ref[...])
    ic_ref[0, 0] = ic[:, 0:512]
    ic_ref[1, 0] = ic[:, 512:1024]
    xg_ref[0] = lora[:, 256:384]
    kb = kr * (1.0 + (0.5 * (ic[:, 0:512] + ic[:, 512:1024]) - 1.0) * ka_ref[...])
    bonus_ref[0] = _head_sum(r * kb * rk_ref[...], gs_ref[...]) * vr


def _front(x, mod, p, cos, sin, gsum):
    B, T, _ = x.shape
    bm = ROW_BLOCK
    hb = bm // HALO
    nh = T // HALO
    row3 = lambda b, i: (b, i, 0)
    const2 = lambda b, i: (0, 0)
    once = pl.Buffered(1)
    cst = lambda shape: pl.BlockSpec(shape, const2, pipeline_mode=once)
    o512 = jax.ShapeDtypeStruct((B, T, 512), f32)
    o2 = jax.ShapeDtypeStruct((2, B, T, 512), f32)
    dir4 = pl.BlockSpec((2, 1, bm, 512), lambda b, i: (0, b, i, 0))
    return pl.pallas_call(
        _front_kernel,
        grid=(B, T // bm),
        in_specs=[pl.BlockSpec((1, bm, D_MODEL), row3),
                  pl.BlockSpec((1, HALO, D_MODEL), lambda b, i: (b, jnp.maximum(i * hb - 1, 0), 0)),
                  pl.BlockSpec((1, HALO, D_MODEL), lambda b, i: (b, jnp.minimum((i + 1) * hb, nh - 1), 0)),
                  pl.BlockSpec((1, 6, D_MODEL), lambda b, i: (b, 0, 0)),
                  cst((1, D_MODEL)), cst((D_MODEL, C_END)),
                  pl.BlockSpec((bm, LANES), lambda b, i: (i, 0)),
                  pl.BlockSpec((bm, LANES), lambda b, i: (i, 0)),
                  cst((1, 512)), cst((1, 256)), cst((256, 256)),
                  cst((3, 1536)), cst((128, 1024)), cst((1, 1024)), cst((128, 1024)), cst((1, 1024)),
                  cst((1, 512)), cst((1, 512)), cst((1, 512))],
        out_specs=[pl.BlockSpec((1, bm, 512), row3)] * 4 + [dir4, dir4]
        + [pl.BlockSpec((1, bm, 512), row3), pl.BlockSpec((1, bm, LANES), row3),
           pl.BlockSpec((1, bm, 512), row3), pl.BlockSpec((1, bm, 512), row3),
           pl.BlockSpec((1, KV_HEADS * VT_ROWS, bm), lambda b, i: (b, 0, i))],
        out_shape=[o512, o512, o512, o512, o2, o2, o512,
                   jax.ShapeDtypeStruct((B, T, LANES), f32),
                   jax.ShapeDtypeStruct((B, T, 512), bf16),
                   jax.ShapeDtypeStruct((B, T, 512), bf16),
                   jax.ShapeDtypeStruct((B, KV_HEADS * VT_ROWS, T), bf16)],
        scratch_shapes=[pltpu.VMEM((bm, KV_HEADS * HEAD_DIM), f32)],
        compiler_params=_cparams(("parallel", "parallel")),
        name="front",
    )(x, x, x, mod, p["norm_mix_g"], p["w_all"], cos, sin, p["qg"], p["kg"], gsum,
      p["conv_w"], p["wd"], p["w0"], p["wi"], p["a0"], p["k_k"], p["k_a"], p["r_k"])


def _scan_kernel(r_ref, k_ref, v_ref, kk_ref, lw_ref, ic_ref, ka_ref, y_ref, h_ref, *, nc):
    L = CHUNK
    d = pl.program_id(0)
    fwd = d == 0

    @pl.when(pl.program_id(2) == 0)
    def _():
        h_ref[...] = jnp.zeros_like(h_ref)

    sgn = jnp.where(fwd, 1, -1)
    col = _iota((L, LANES), 1)
    row = _iota((L, LANES), 0)
    order = ((col & (L - 1)) - row) * sgn
    left = col < L
    strict_l = (order < 0) & left
    strict_r = (order < 0) & jnp.logical_not(left)
    incl = order <= 0
    eye_r = jnp.where(col - L == row, 1.0, 0.0)
    right_f = jnp.where(left, 0.0, 1.0)
    incl_bf = jnp.where(order[:, 0:L] <= 0, 1.0, 0.0).astype(bf16)
    lane = _iota((1, LANES), 1)
    head_m = (jnp.where(lane < L, 1.0, 0.0), jnp.where(lane < L, 0.0, 1.0))
    blockdiag = (_iota((LANES, LANES), 0) < L) == (_iota((LANES, LANES), 1) < L)
    eye128 = _iota((LANES, LANES), 0) == _iota((LANES, LANES), 1)
    z128 = jnp.zeros((L, LANES), f32)
    z256 = jnp.zeros((L, 2 * LANES), f32)
    ka = ka_ref[...]
    pairs = [(i, p) for i in range(nc) for p in range(4)]
    units = [(i, p, hh) for (i, p) in pairs for hh in range(2)]

    rows_of, ch = [], []
    for i in range(nc):
        ci = jnp.where(fwd, i, nc - 1 - i)
        rows = pl.ds(pl.multiple_of(ci * L, L), L)
        rows_of.append(rows)
        lw = lw_ref[0, 0, rows, :]
        ic = ic_ref[0, 0, rows, :]
        kk = kk_ref[0, rows, :]
        kd = k_ref[0, rows, :] * (1.0 + (ic - 1.0) * ka)
        b = kk * ic
        l1 = lw.astype(bf16)
        e1 = lw - l1.astype(f32)
        l2 = e1.astype(bf16)
        l3 = (e1 - l2.astype(f32)).astype(bf16)
        cum = (jnp.dot(incl_bf, l1, preferred_element_type=f32)
               + jnp.dot(incl_bf, l2, preferred_element_type=f32)
               + jnp.dot(incl_bf, l3, preferred_element_type=f32))
        tot = jnp.where(fwd, cum[L - 1:L, :], cum[0:1, :])
        p_inv = jnp.exp(-cum)
        p_end = jnp.exp(tot - cum)
        ch.append(dict(
            v=v_ref[0, rows, :], p_tot=jnp.exp(tot),
            rt=r_ref[0, rows, :] * jnp.exp(cum), at=-kk * jnp.exp(cum - lw),
            bt=b * p_inv, kt=kd * p_inv, bh=b * p_end, kh=kd * p_end))

    def sl(i, p, name):
        return ch[i][name][:, p * LANES:(p + 1) * LANES]

    am, vm, g = {}, {}, {}
    for (i, p) in pairs:
        z = jnp.concatenate([sl(i, p, "bt"), sl(i, p, "kt")], axis=0).astype(bf16)
        for hh in range(2):
            u = (i, p, hh)
            am[u] = sl(i, p, "at") * head_m[hh]
            vm[u] = sl(i, p, "v") * head_m[hh]
            xm = jnp.concatenate([am[u], sl(i, p, "rt") * head_m[hh]], axis=0)
            g[u] = _dot_nt(xm, z)
    pk, w1 = {}, {}
    for u in units:
        gt = g[u][0:L]
        pk[u] = jnp.where(strict_l, gt, 0.0) + eye_r
        w1[u] = _dot(jnp.where(strict_r, gt, 0.0), jnp.concatenate([z128, vm[u]], axis=0))
    for _ in range(6):
        for u in units:
            prod = _dot(pk[u], jnp.concatenate([pk[u], z128], axis=0))
            pk[u] = prod + pk[u] * right_f
    tu, ry = {}, {}
    for u in units:
        tu[u] = _dot(pk[u], jnp.concatenate([z256, jnp.concatenate([am[u], w1[u]], axis=1)], axis=0))
    for u in units:
        lhs = jnp.where(incl, g[u][L:2 * L], 0.0)
        ry[u] = _dot(lhs, jnp.concatenate([tu[u], jnp.concatenate([z128, vm[u]], axis=1)], axis=0))
    r_pair, y_pair, m_p, c_p, p_col = {}, {}, {}, {}, {}
    for (i, p) in pairs:
        u0, u1 = (i, p, 0), (i, p, 1)
        t2 = tu[u0] + tu[u1]
        r2 = ry[u0] + ry[u1]
        r_pair[i, p] = sl(i, p, "rt") + r2[:, 0:LANES]
        y_pair[i, p] = r2[:, LANES:]
        zl = jnp.concatenate([sl(i, p, "bh"), sl(i, p, "kh")], axis=0)
        zr = jnp.concatenate([t2, jnp.concatenate([z128, sl(i, p, "v")], axis=1)], axis=0)
        mc = _dot_tn(zl, zr)
        m_p[i, p] = jnp.where(blockdiag, mc[:, 0:LANES], 0.0)
        c_p[i, p] = jnp.where(blockdiag, mc[:, LANES:], 0.0)
        p_col[i, p] = jnp.sum(jnp.where(eye128, sl(i, p, "p_tot"), 0.0), axis=1, keepdims=True)
    hs = [h_ref[p] for p in range(4)]
    for i in range(nc):
        for p in range(4):
            sd = _dot(jnp.concatenate([r_pair[i, p], m_p[i, p]], axis=0), hs[p])
            y_ref[0, 0, rows_of[i], p * LANES:(p + 1) * LANES] = sd[0:L] + y_pair[i, p]
            hs[p] = p_col[i, p] * hs[p] + sd[L:] + c_p[i, p]
    for p in range(4):
        h_ref[p] = hs[p]


def _rwkv_scan(r, k, v, kk, lw, ic, k_a):
    B, T, _ = r.shape
    nc = SCAN_CHUNKS
    lb = nc * CHUNK
    ns = T // lb

    def blk(d, b, s):
        return jnp.where(d == 0, s, ns - 1 - s)

    shared = pl.BlockSpec((1, lb, 512), lambda d, b, s: (b, blk(d, b, s), 0))
    perdir = pl.BlockSpec((1, 1, lb, 512), lambda d, b, s: (d, b, blk(d, b, s), 0))
    return pl.pallas_call(
        functools.partial(_scan_kernel, nc=nc),
        grid=(2, B, ns),
        in_specs=[shared, shared, shared, shared, perdir, perdir,
                  pl.BlockSpec((1, 512), lambda d, b, s: (0, 0))],
        out_specs=perdir,
        out_shape=jax.ShapeDtypeStruct((2, B, T, 512), f32),
        scratch_shapes=[pltpu.VMEM((4, LANES, LANES), f32)],
        compiler_params=_cparams(("arbitrary", "arbitrary", "arbitrary")),
        name="rwkv_scan",
    )(r, k, v, kk, lw, ic, k_a)


def _attn_kernel(q_ref, k_ref, vt_ref, o_ref, m_ref, acc_ref, *scr):
    ki = pl.program_id(3)
    bk = k_ref.shape[1]

    @pl.when(ki == 0)
    def _():
        m_ref[...] = jnp.full_like(m_ref, -1e30)
        acc_ref[...] = jnp.zeros_like(acc_ref)

    bq = q_ref.shape[1]
    units = [(j, hh, c0) for c0 in range(0, bq, ATTN_QSUB) for j in range(2) for hh in range(2)]
    n = len(units)
    nb = len(scr) // 2
    s_refs = [scr[u % nb] for u in range(n)]
    p_refs = [scr[nb + u % nb] for u in range(n)]
    first = ki == 0
    m_old, m_new = [None] * n, [None] * n

    def scores(u):
        j, hh, c0 = units[u]
        s_refs[u][...] = lax.dot_general(k_ref[0, :, hh * LANES:(hh + 1) * LANES],
                                         q_ref[0, c0:c0 + ATTN_QSUB, j * LANES:(j + 1) * LANES],
                                         (((1,), (1,)), ((), ())), preferred_element_type=f32)

    def softmax(u):
        j, hh, c0 = units[u]
        m_old[u] = m_ref[2 * j + hh, :, c0:c0 + ATTN_QSUB]
        c = jnp.where(first, 0.0, m_old[u])
        smax = jnp.max(s_refs[u][...], axis=0, keepdims=True)
        t = jnp.maximum(jnp.where(first, -1e30, 0.0), smax - c).astype(bf16)
        m_new[u] = c + t.astype(f32)
        for r0 in range(0, bk, ATTN_STRIP):
            rs = slice(r0, r0 + ATTN_STRIP)
            p_refs[u][rs, :] = jnp.exp2((s_refs[u][rs, :] - c).astype(bf16) - t)

    def update(u):
        j, hh, c0 = units[u]
        i = 2 * j + hh
        pv = jnp.dot(vt_ref[0], p_refs[u][...], preferred_element_type=f32)
        acc_ref[i, :, c0:c0 + ATTN_QSUB] = (acc_ref[i, :, c0:c0 + ATTN_QSUB]
                                            * jnp.exp2(m_old[u] - m_new[u]) + pv)
        m_ref[i, :, c0:c0 + ATTN_QSUB] = m_new[u]

    scores(0)
    for u in range(n):
        if u + 1 < n:
            scores(u + 1)
        softmax(u)
        if u > 0:
            update(u - 1)
    update(n - 1)

    @pl.when(ki == pl.num_programs(3) - 1)
    def _():
        for j in range(2):
            o = jnp.concatenate([acc_ref[i, 0:HEAD_DIM, :] * (1.0 / acc_ref[i, HEAD_DIM:HEAD_DIM + 1, :])
                                 for i in (2 * j, 2 * j + 1)], axis=0)
            o_ref[0, :, j * LANES:(j + 1) * LANES] = o.T.astype(o_ref.dtype)


def _attention(q, k, vt):
    B, T, _ = q.shape
    bq, bk = ATTN_BQ, ATTN_BK
    nb = ATTN_BUFS
    return pl.pallas_call(
        _attn_kernel,
        grid=(B, KV_HEADS, T // bq, T // bk),
        in_specs=[pl.BlockSpec((1, bq, 256), lambda b, g, qi, ki: (b, qi, g)),
                  pl.BlockSpec((1, bk, 256), lambda b, g, qi, ki: (b, ki, g)),
                  pl.BlockSpec((1, VT_ROWS, bk), lambda b, g, qi, ki: (b, g, ki))],
        out_specs=pl.BlockSpec((1, bq, 256), lambda b, g, qi, ki: (b, qi, g)),
        out_shape=jax.ShapeDtypeStruct((B, T, ATTN_DIM), bf16),
        scratch_shapes=[pltpu.VMEM((4, 1, bq), f32), pltpu.VMEM((4, VT_ROWS, bq), f32)]
        + [pltpu.VMEM((bk, ATTN_QSUB), f32)] * nb + [pltpu.VMEM((bk, ATTN_QSUB), bf16)] * nb,
        compiler_params=_cparams(("parallel", "parallel", "parallel", "arbitrary")),
        name="gqa_attention",
    )(q, k, vt)


def _back_kernel(x_ref, y_ref, bonus_ref, xg_ref, attn_ref, mod_ref, gs_ref, gup_ref, wo_ref, lng_ref,
                 lnb_ref, ng_ref, wg_ref, wu_ref, wd_ref, o_ref):
    gs = gs_ref[...]
    y = y_ref[0, 0] + y_ref[1, 0]
    mu = _head_sum(y, gs) * (1.0 / HEAD_DIM)
    yc = y - mu
    var = _head_sum(yc * yc, gs) * (1.0 / HEAD_DIM)
    yn = yc * lax.rsqrt(var + GN_EPS) * lng_ref[...] + lnb_ref[...]
    g = _dot(_sigmoid(xg_ref[0]), gup_ref[...])
    rw = ((yn + bonus_ref[0]) * g).astype(bf16)
    mix = (jnp.dot(rw, wo_ref[0:RWKV_DIM, :], preferred_element_type=f32)
           + jnp.dot(attn_ref[0], wo_ref[RWKV_DIM:, :], preferred_element_type=f32))
    x = x_ref[0] + mod_ref[0, 2:3, :] * mix

    ms = jnp.mean(x * x, axis=-1, keepdims=True)
    h = x * lax.rsqrt(ms + NORM_EPS) * ng_ref[...]
    h = (h * (1.0 + mod_ref[0, 4:5, :]) + mod_ref[0, 3:4, :]).astype(bf16)
    gt = jnp.dot(h, wg_ref[...], preferred_element_type=f32)
    up = jnp.dot(h, wu_ref[...], preferred_element_type=f32)
    act = (gt * _sigmoid(gt) * up).astype(bf16)
    o_ref[0] = x + mod_ref[0, 5:6, :] * jnp.dot(act, wd_ref[...], preferred_element_type=f32)


def _back(x, y, bonus, xg, attn, mod, gsum, p):
    B, T, _ = x.shape
    bm = FFN_ROWS
    row3 = lambda b, i: (b, i, 0)
    const2 = lambda b, i: (0, 0)
    once = pl.Buffered(1)
    cst = lambda shape, imap=const2: pl.BlockSpec(shape, imap, pipeline_mode=once)
    return pl.pallas_call(
        _back_kernel,
        grid=(B, T // bm),
        in_specs=[pl.BlockSpec((1, bm, D_MODEL), row3),
                  pl.BlockSpec((2, 1, bm, 512), lambda b, i: (0, b, i, 0)),
                  pl.BlockSpec((1, bm, 512), row3),
                  pl.BlockSpec((1, bm, LANES), row3),
                  pl.BlockSpec((1, bm, 512), row3),
                  pl.BlockSpec((1, 6, D_MODEL), lambda b, i: (b, 0, 0)),
                  cst((256, 256)), cst((128, 512)), cst((D_MODEL, D_MODEL)), cst((1, 512)), cst((1, 512)),
                  cst((1, D_MODEL)),
                  cst((D_MODEL, D_FF)), cst((D_MODEL, D_FF), lambda b, i: (0, 1)), cst((D_FF, D_MODEL))],
        out_specs=pl.BlockSpec((1, bm, D_MODEL), row3),
        out_shape=jax.ShapeDtypeStruct((B, T, D_MODEL), f32),
        compiler_params=_cparams(("parallel", "parallel"), BACK_VMEM_LIMIT),
        name="back",
    )(x, y, bonus, xg, attn, mod, gsum, p["gate_up"], p["w_out"], p["ln_g"], p["ln_b"],
      p["norm_ffn_g"], p["w_ffn_in"], p["w_ffn_in"], p["w_ffn_out"])


def _rope_tables(T):
    rows = T // GRID_W
    quarter = HEAD_DIM // 4
    freq = 1.0 / (ROPE_THETA ** (np.arange(quarter, dtype=np.float64) / quarter))
    ang_r = np.arange(rows, dtype=np.float64)[:, None] * freq[None, :]
    ang_c = np.arange(GRID_W, dtype=np.float64)[:, None] * freq[None, :]

    def table(fr, fc):
        r = jnp.broadcast_to(jnp.asarray(fr, f32)[:, None, :], (rows, GRID_W, 2 * quarter))
        c = jnp.broadcast_to(jnp.asarray(fc, f32)[None, :, :], (rows, GRID_W, 2 * quarter))
        t = jnp.concatenate([r, c], axis=-1).reshape(T, HEAD_DIM)
        return jnp.concatenate([t, t], axis=1)

    cos = table(np.concatenate([np.cos(ang_r)] * 2, axis=1), np.concatenate([np.cos(ang_c)] * 2, axis=1))
    sin = table(np.concatenate([-np.sin(ang_r), np.sin(ang_r)], axis=1),
                np.concatenate([-np.sin(ang_c), np.sin(ang_c)], axis=1))
    return cos, sin


def _block_lora(up):
    z = jnp.zeros_like(up[0])
    return jnp.concatenate([jnp.concatenate([up[0], z], axis=1),
                            jnp.concatenate([z, up[1]], axis=1)], axis=0)


def _layer_params(l, ada_w, ada_b, norm_mix_g, norm_ffn_g, w_in, conv_w, decay_w0, decay_up, iclr_a0,
                  iclr_up, gate_up, k_k, k_a, r_k, ln_x_g, ln_x_b, q_norm_g, k_norm_g, w_out,
                  w_ffn_in, w_ffn_out):
    wi = w_in[l].astype(bf16)
    wk = wi[:, 2048:2176]
    wv = wi[:, 2176:2304]
    wk2 = jnp.concatenate([wk[:, 0:64], wk[:, 0:64], wk[:, 64:128], wk[:, 64:128]], axis=1)
    w_all = jnp.concatenate([wi[:, 0:1536], wi[:, 2304:2688], wi[:, 1536:2048], wk2, wv], axis=1)
    return dict(
        ada_w=ada_w[l], ada_b=ada_b[l],
        norm_mix_g=norm_mix_g[l].reshape(1, -1), norm_ffn_g=norm_ffn_g[l].reshape(1, -1),
        w_all=w_all, conv_w=conv_w[l],
        wd=_block_lora(decay_up[l]).astype(bf16), w0=decay_w0[l].reshape(1, -1),
        wi=_block_lora(iclr_up[l]).astype(bf16), a0=iclr_a0[l].reshape(1, -1),
        gate_up=gate_up[l].astype(bf16),
        k_k=k_k[l].reshape(1, -1), k_a=k_a[l].reshape(1, -1), r_k=r_k[l].reshape(1, -1),
        ln_g=ln_x_g[l].reshape(1, -1), ln_b=ln_x_b[l].reshape(1, -1),
        qg=jnp.tile(q_norm_g[l], 8).reshape(1, -1), kg=jnp.tile(k_norm_g[l], 4).reshape(1, -1),
        w_out=w_out[l].astype(bf16), w_ffn_in=w_ffn_in[l].astype(bf16),
        w_ffn_out=w_ffn_out[l].astype(bf16))


def _layer(x, c, p, cos, sin, gsum):
    mod = _modulation(c, p["ada_w"], p["ada_b"])
    r, kr, vr, kk, lw, ic, bonus, xg, q, k, vt = _front(x, mod, p, cos, sin, gsum)
    y = _rwkv_scan(r, kr, vr, kk, lw, ic, p["k_a"])
    attn = _attention(q, k, vt)
    return _back(x, y, bonus, xg, attn, mod, gsum, p)


def kernel(x_prompt, x_sample, c_prompt, c_sample, ada_w, ada_b, norm_mix_g, norm_ffn_g, w_in, conv_w,
           decay_w0, decay_up, iclr_a0, iclr_up, gate_up, k_k, k_a, r_k, ln_x_g, ln_x_b, q_norm_g,
           k_norm_g, w_out, w_ffn_in, w_ffn_out):
    depth = ada_w.shape[0]
    params = [_layer_params(l, ada_w, ada_b, norm_mix_g, norm_ffn_g, w_in, conv_w, decay_w0, decay_up,
                            iclr_a0, iclr_up, gate_up, k_k, k_a, r_k, ln_x_g, ln_x_b, q_norm_g,
                            k_norm_g, w_out, w_ffn_in, w_ffn_out) for l in range(depth)]
    head = np.arange(256) // HEAD_DIM
    gsum = jnp.asarray(head[:, None] == head[None, :], dtype=bf16)

    def run_trunk(x, c):
        cos, sin = _rope_tables(x.shape[1])
        for p in params:
            x = _layer(x, c, p, cos, sin, gsum)
        return x

    return (run_trunk(x_prompt, c_prompt), run_trunk(x_sample, c_sample))
```

```python
import functools
import math

import jax
import jax.numpy as jnp
import numpy as np
from jax import lax
from jax.experimental import pallas as pl
from jax.experimental.pallas import tpu as pltpu

f32 = jnp.float32
bf16 = jnp.bfloat16

D_MODEL = 1024
HEAD_DIM = 64
RWKV_DIM = 512
ATTN_DIM = 512
KV_HEADS = 2
LORA = 64
D_FF = 2816
GRID_W = 64
ROPE_THETA = 10000.0
NORM_EPS = 1e-6
QK_EPS = 1e-6
GN_EPS = 64e-5
DECAY_SCALE = math.exp(-0.5)

LANES = 128
VMEM_LIMIT = 48 * 1024 * 1024
BACK_VMEM_LIMIT = 56 * 1024 * 1024

ROW_BLOCK = 512
HALO = 16
FFN_ROWS = 512
CHUNK = 64
SCAN_CHUNKS = 4
ATTN_BQ = 1024
ATTN_BK = 4096
ATTN_BUFS = 3
ATTN_QSUB = 256
ATTN_STRIP = 16
LOG2E = 1.4426950408889634
VT_ROWS = 80

C_RKV = 0
C_LORA = 1536
C_Q = 1920
C_K = 2432
C_V = 2688
C_END = 2816


def _cparams(sem, vmem_limit=None):
    return pltpu.CompilerParams(dimension_semantics=sem, vmem_limit_bytes=vmem_limit or VMEM_LIMIT)


def _dot(a, b):
    return jnp.dot(a.astype(bf16), b.astype(bf16), preferred_element_type=f32)


def _dot_nt(a, b):
    return lax.dot_general(a.astype(bf16), b.astype(bf16), (((1,), (1,)), ((), ())),
                           preferred_element_type=f32)


def _dot_tn(a, b):
    return lax.dot_general(a.astype(bf16), b.astype(bf16), (((0,), (0,)), ((), ())),
                           preferred_element_type=f32)


def _split2(a):
    hi = a.astype(bf16)
    lo = (a - hi.astype(f32)).astype(bf16)
    return hi, lo


def _head_sum(a, g):
    hi, lo = _split2(a)
    w = g.shape[0]
    return jnp.concatenate([jnp.dot(hi[:, c:c + w], g, preferred_element_type=f32)
                            + jnp.dot(lo[:, c:c + w], g, preferred_element_type=f32)
                            for c in range(0, a.shape[1], w)], axis=1)


def _sigmoid(x):
    return 1.0 / (1.0 + jnp.exp(-x))


def _iota(shape, dim):
    return lax.broadcasted_iota(jnp.int32, shape, dim)


def _mod_kernel(c_ref, w_ref, b_ref, o_ref):
    c = c_ref[...]
    s = c * _sigmoid(c)
    sh, sl = _split2(s)
    wh, wl = _split2(w_ref[...])
    acc = jnp.dot(sh, wh, preferred_element_type=f32)
    acc += jnp.dot(sh, wl, preferred_element_type=f32)
    acc += jnp.dot(sl, wh, preferred_element_type=f32)
    o_ref[...] = acc + b_ref[...]


def _modulation(c, ada_w, ada_b):
    B = c.shape[0]
    Bp = max(8, B)
    cp = jnp.pad(c, ((0, Bp - B), (0, 0)))
    n = ada_w.shape[1] // D_MODEL
    out = pl.pallas_call(
        _mod_kernel,
        grid=(n,),
        in_specs=[pl.BlockSpec((Bp, D_MODEL), lambda j: (0, 0)),
                  pl.BlockSpec((D_MODEL, D_MODEL), lambda j: (0, j)),
                  pl.BlockSpec((1, D_MODEL), lambda j: (0, j))],
        out_specs=pl.BlockSpec((Bp, D_MODEL), lambda j: (0, j)),
        out_shape=jax.ShapeDtypeStruct((Bp, n * D_MODEL), f32),
        compiler_params=_cparams(("arbitrary",)),
        name="adaln_mod",
    )(cp, ada_w, ada_b.reshape(1, -1))
    return out[:B].reshape(B, n, D_MODEL)


def _rope(x, cos, sin):
    w = x.shape[1]
    up = pltpu.roll(x, w - 16, 1)
    dn = pltpu.roll(x, 16, 1)
    first = (_iota((1, w), 1) & 16) == 0
    return x * cos + jnp.where(first, up, dn) * sin


def _qk_norm(x, g_ref, gain):
    ms = _head_sum(x * x, g_ref[...]) * (1.0 / HEAD_DIM)
    return x * lax.rsqrt(ms + QK_EPS) * gain


def _front_kernel(x_ref, xp_ref, xn_ref, mod_ref, ng_ref, w_ref, cos_ref, sin_ref, qg_ref, kg_ref, gs_ref,
                  cw_ref, wd_ref, w0_ref, wi_ref, a0_ref, kk_ref, ka_ref, rk_ref,
                  r_ref, k_ref, v_ref, kkn_ref, lw_ref, ic_ref, bonus_ref, xg_ref, q_ref, ka_out_ref, vt_ref,
                  vs_ref):
    i = pl.program_id(1)
    n = pl.num_programs(1)
    bm = x_ref.shape[1]
    halo = xp_ref.shape[1]

    def ada_norm(x):
        ms = jnp.mean(x * x, axis=-1, keepdims=True)
        h = x * lax.rsqrt(ms + NORM_EPS) * ng_ref[...]
        return h * (1.0 + mod_ref[0, 1:2, :]) + mod_ref[0, 0:1, :]

    h_cur = ada_norm(x_ref[0])
    h_ext = jnp.concatenate([ada_norm(xp_ref[0]) * jnp.where(i > 0, 1.0, 0.0), h_cur,
                             ada_norm(xn_ref[0]) * jnp.where(i < n - 1, 1.0, 0.0)], axis=0)
    rkv_ext = jnp.dot(h_ext.astype(bf16), w_ref[:, C_RKV:C_LORA], preferred_element_type=f32)
    hb = h_cur.astype(bf16)
    lora = jnp.dot(hb, w_ref[:, C_LORA:C_Q], preferred_element_type=f32)
    q = jnp.dot(hb, w_ref[:, C_Q:C_K], preferred_element_type=f32)
    k = jnp.dot(hb, w_ref[:, C_K:C_V], preferred_element_type=f32)
    vs_ref[...] = jnp.dot(hb, w_ref[:, C_V:C_END], preferred_element_type=f32)

    vt = vs_ref[...].T.astype(bf16)
    pad = jnp.where(_iota((VT_ROWS - HEAD_DIM, vt.shape[1]), 0) == 0, 1.0, 0.0).astype(bf16)
    for g in range(KV_HEADS):
        vt_ref[0, g * VT_ROWS:g * VT_ROWS + HEAD_DIM, :] = vt[g * HEAD_DIM:(g + 1) * HEAD_DIM]
        vt_ref[0, g * VT_ROWS + HEAD_DIM:(g + 1) * VT_ROWS, :] = pad
    cos = cos_ref[...]
    sin = sin_ref[...]
    cos4 = jnp.concatenate([cos] * 4, axis=1)
    sin4 = jnp.concatenate([sin] * 4, axis=1)
    qn = _rope(_qk_norm(q, gs_ref, qg_ref[...]), cos4, sin4)
    q_ref[0] = (qn * (LOG2E * HEAD_DIM ** -0.5)).astype(bf16)
    kn = _rope(_qk_norm(k, gs_ref, kg_ref[...]), cos4[:, 0:256], sin4[:, 0:256])
    half0 = _iota((1, LANES), 1) < HEAD_DIM
    ka_out_ref[0] = jnp.concatenate([jnp.where(keep, kn[:, g * LANES:(g + 1) * LANES], 0.0)
                                     for g in range(KV_HEADS) for keep in (half0, jnp.logical_not(half0))],
                                    axis=1).astype(bf16)

    next_shift = bm + 2 * halo - 1
    outs = (r_ref, k_ref, v_ref)
    rkv = []
    for p in range(3):
        cols = slice(p * 512, (p + 1) * 512)
        ext = rkv_ext[:, cols]
        y = (cw_ref[0:1, cols] * pltpu.roll(ext, 1, 0)[halo:halo + bm]
             + cw_ref[1:2, cols] * ext[halo:halo + bm]
             + cw_ref[2:3, cols] * pltpu.roll(ext, next_shift, 0)[halo:halo + bm])
        outs[p][0] = y
        rkv.append(y)
    r, kr, vr = rkv
    kkh = kr * kk_ref[...]
    kkn_ref[0] = kkh * lax.rsqrt(_head_sum(kkh * kkh, gs_ref[...]) + 1e-12)
    dl = _dot(jnp.tanh(lora[:, 0:128]), wd_ref[...]) + w0_ref[...]
    lw = -DECAY_SCALE * _sigmoid(dl)
    lw_ref[0, 0] = lw[:, 0:512]
    lw_ref[1, 0] = lw[:, 512:1024]
    ic = _sigmoid(_dot(lora[:, 128:256], wi_ref[...]) + a0_ref[...])
    ic_ref[0, 0] = ic[:, 0:512]
    ic_ref[1, 0] = ic[:, 512:1024]
    xg_ref[0] = lora[:, 256:384]
    kb = kr * (1.0 + (0.5 * (ic[:, 0:512] + ic[:, 512:1024]) - 1.0) * ka_ref[...])
    bonus_ref[0] = _head_sum(r * kb * rk_ref[...], gs_ref[...]) * vr


def _front(x, mod, p, cos, sin, gsum):
    B, T, _ = x.shape
    bm = ROW_BLOCK
    hb = bm // HALO
    nh = T // HALO
    row3 = lambda b, i: (b, i, 0)
    const2 = lambda b, i: (0, 0)
    once = pl.Buffered(1)
    cst = lambda shape: pl.BlockSpec(shape, const2, pipeline_mode=once)
    o512 = jax.ShapeDtypeStruct((B, T, 512), f32)
    o2 = jax.ShapeDtypeStruct((2, B, T, 512), f32)
    dir4 = pl.BlockSpec((2, 1, bm, 512), lambda b, i: (0, b, i, 0))
    return pl.pallas_call(
        _front_kernel,
        grid=(B, T // bm),
        in_specs=[pl.BlockSpec((1, bm, D_MODEL), row3),
                  pl.BlockSpec((1, HALO, D_MODEL), lambda b, i: (b, jnp.maximum(i * hb - 1, 0), 0)),
                  pl.BlockSpec((1, HALO, D_MODEL), lambda b, i: (b, jnp.minimum((i + 1) * hb, nh - 1), 0)),
                  pl.BlockSpec((1, 6, D_MODEL), lambda b, i: (b, 0, 0)),
                  cst((1, D_MODEL)), cst((D_MODEL, C_END)),
                  pl.BlockSpec((bm, LANES), lambda b, i: (i, 0)),
                  pl.BlockSpec((bm, LANES), lambda b, i: (i, 0)),
                  cst((1, 512)), cst((1, 256)), cst((256, 256)),
                  cst((3, 1536)), cst((128, 1024)), cst((1, 1024)), cst((128, 1024)), cst((1, 1024)),
                  cst((1, 512)), cst((1, 512)), cst((1, 512))],
        out_specs=[pl.BlockSpec((1, bm, 512), row3)] * 4 + [dir4, dir4]
        + [pl.BlockSpec((1, bm, 512), row3), pl.BlockSpec((1, bm, LANES), row3),
           pl.BlockSpec((1, bm, 512), row3), pl.BlockSpec((1, bm, 512), row3),
           pl.BlockSpec((1, KV_HEADS * VT_ROWS, bm), lambda b, i: (b, 0, i))],
        out_shape=[o512, o512, o512, o512, o2, o2, o512,
                   jax.ShapeDtypeStruct((B, T, LANES), f32),
                   jax.ShapeDtypeStruct((B, T, 512), bf16),
                   jax.ShapeDtypeStruct((B, T, 512), bf16),
                   jax.ShapeDtypeStruct((B, KV_HEADS * VT_ROWS, T), bf16)],
        scratch_shapes=[pltpu.VMEM((bm, KV_HEADS * HEAD_DIM), f32)],
        compiler_params=_cparams(("parallel", "parallel")),
        name="front",
    )(x, x, x, mod, p["norm_mix_g"], p["w_all"], cos, sin, p["qg"], p["kg"], gsum,
      p["conv_w"], p["wd"], p["w0"], p["wi"], p["a0"], p["k_k"], p["k_a"], p["r_k"])


def _scan_kernel(r_ref, k_ref, v_ref, kk_ref, lw_ref, ic_ref, ka_ref, y_ref, h_ref, *, nc):
    L = CHUNK
    d = pl.program_id(0)
    fwd = d == 0

    @pl.when(pl.program_id(2) == 0)
    def _():
        h_ref[...] = jnp.zeros_like(h_ref)

    sgn = jnp.where(fwd, 1, -1)
    col = _iota((L, LANES), 1)
    row = _iota((L, LANES), 0)
    order = ((col & (L - 1)) - row) * sgn
    left = col < L
    strict_l = (order < 0) & left
    strict_r = (order < 0) & jnp.logical_not(left)
    incl = order <= 0
    eye_r = jnp.where(col - L == row, 1.0, 0.0)
    right_f = jnp.where(left, 0.0, 1.0)
    incl_bf = jnp.where(order[:, 0:L] <= 0, 1.0, 0.0).astype(bf16)
    lane = _iota((1, LANES), 1)
    head_m = (jnp.where(lane < L, 1.0, 0.0), jnp.where(lane < L, 0.0, 1.0))
    blockdiag = (_iota((LANES, LANES), 0) < L) == (_iota((LANES, LANES), 1) < L)
    eye128 = _iota((LANES, LANES), 0) == _iota((LANES, LANES), 1)
    z128 = jnp.zeros((L, LANES), f32)
    z256 = jnp.zeros((L, 2 * LANES), f32)
    ka = ka_ref[...]
    pairs = [(i, p) for i in range(nc) for p in range(4)]
    units = [(i, p, hh) for (i, p) in pairs for hh in range(2)]

    rows_of, ch = [], []
    for i in range(nc):
        ci = jnp.where(fwd, i, nc - 1 - i)
        rows = pl.ds(pl.multiple_of(ci * L, L), L)
        rows_of.append(rows)
        lw = lw_ref[0, 0, rows, :]
        ic = ic_ref[0, 0, rows, :]
        kk = kk_ref[0, rows, :]
        kd = k_ref[0, rows, :] * (1.0 + (ic - 1.0) * ka)
        b = kk * ic
        l1 = lw.astype(bf16)
        e1 = lw - l1.astype(f32)
        l2 = e1.astype(bf16)
        l3 = (e1 - l2.astype(f32)).astype(bf16)
        cum = (jnp.dot(incl_bf, l1, preferred_element_type=f32)
               + jnp.dot(incl_bf, l2, preferred_element_type=f32)
               + jnp.dot(incl_bf, l3, preferred_element_type=f32))
        tot = jnp.where(fwd, cum[L - 1:L, :], cum[0:1, :])
        p_inv = jnp.exp(-cum)
        p_end = jnp.exp(tot - cum)
        ch.append(dict(
            v=v_ref[0, rows, :], p_tot=jnp.exp(tot),
            rt=r_ref[0, rows, :] * jnp.exp(cum), at=-kk * jnp.exp(cum - lw),
            bt=b * p_inv, kt=kd * p_inv, bh=b * p_end, kh=kd * p_end))

    def sl(i, p, name):
        return ch[i][name][:, p * LANES:(p + 1) * LANES]

    am, vm, g = {}, {}, {}
    for (i, p) in pairs:
        z = jnp.concatenate([sl(i, p, "bt"), sl(i, p, "kt")], axis=0).astype(bf16)
        for hh in range(2):
            u = (i, p, hh)
            am[u] = sl(i, p, "at") * head_m[hh]
            vm[u] = sl(i, p, "v") * head_m[hh]
            xm = jnp.concatenate([am[u], sl(i, p, "rt") * head_m[hh]], axis=0)
            g[u] = _dot_nt(xm, z)
    pk, w1 = {}, {}
    for u in units:
        gt = g[u][0:L]
        pk[u] = jnp.where(strict_l, gt, 0.0) + eye_r
        w1[u] = _dot(jnp.where(strict_r, gt, 0.0), jnp.concatenate([z128, vm[u]], axis=0))
    for _ in range(6):
        for u in units:
            prod = _dot(pk[u], jnp.concatenate([pk[u], z128], axis=0))
            pk[u] = prod + pk[u] * right_f
    tu, ry = {}, {}
    for u in units:
        tu[u] = _dot(pk[u], jnp.concatenate([z256, jnp.concatenate([am[u], w1[u]], axis=1)], axis=0))
    for u in units:
        lhs = jnp.where(incl, g[u][L:2 * L], 0.0)
        ry[u] = _dot(lhs, jnp.concatenate([tu[u], jnp.concatenate([z128, vm[u]], axis=1)], axis=0))
    r_pair, y_pair, m_p, c_p, p_col = {}, {}, {}, {}, {}
    for (i, p) in pairs:
        u0, u1 = (i, p, 0), (i, p, 1)
        t2 = tu[u0] + tu[u1]
        r2 = ry[u0] + ry[u1]
        r_pair[i, p] = sl(i, p, "rt") + r2[:, 0:LANES]
        y_pair[i, p] = r2[:, LANES:]
        zl = jnp.concatenate([sl(i, p, "bh"), sl(i, p, "kh")], axis=0)
        zr = jnp.concatenate([t2, jnp.concatenate([z128, sl(i, p, "v")], axis=1)], axis=0)
        mc = _dot_tn(zl, zr)
        m_p[i, p] = jnp.where(blockdiag, mc[:, 0:LANES], 0.0)
        c_p[i, p] = jnp.where(blockdiag, mc[:, LANES:], 0.0)
        p_col[i, p] = jnp.sum(jnp.where(eye128, sl(i, p, "p_tot"), 0.0), axis=1, keepdims=True)
    hs = [h_ref[p] for p in range(4)]
    for i in range(nc):
        for p in range(4):
            sd = _dot(jnp.concatenate([r_pair[i, p], m_p[i, p]], axis=0), hs[p])
            y_ref[0, 0, rows_of[i], p * LANES:(p + 1) * LANES] = sd[0:L] + y_pair[i, p]
            hs[p] = p_col[i, p] * hs[p] + sd[L:] + c_p[i, p]
    for p in range(4):
        h_ref[p] = hs[p]


def _rwkv_scan(r, k, v, kk, lw, ic, k_a):
    B, T, _ = r.shape
    nc = SCAN_CHUNKS
    lb = nc * CHUNK
    ns = T // lb

    def blk(d, b, s):
        return jnp.where(d == 0, s, ns - 1 - s)

    shared = pl.BlockSpec((1, lb, 512), lambda d, b, s: (b, blk(d, b, s), 0))
    perdir = pl.BlockSpec((1, 1, lb, 512), lambda d, b, s: (d, b, blk(d, b, s), 0))
    return pl.pallas_call(
        functools.partial(_scan_kernel, nc=nc),
        grid=(2, B, ns),
        in_specs=[shared, shared, shared, shared, perdir, perdir,
                  pl.BlockSpec((1, 512), lambda d, b, s: (0, 0))],
        out_specs=perdir,
        out_shape=jax.ShapeDtypeStruct((2, B, T, 512), f32),
        scratch_shapes=[pltpu.VMEM((4, LANES, LANES), f32)],
        compiler_params=_cparams(("arbitrary", "arbitrary", "arbitrary")),
        name="rwkv_scan",
    )(r, k, v, kk, lw, ic, k_a)


def _attn_kernel(q_ref, k_ref, vt_ref, o_ref, m_ref, acc_ref, *scr):
    ki = pl.program_id(3)
    bk = k_ref.shape[1]

    @pl.when(ki == 0)
    def _():
        m_ref[...] = jnp.full_like(m_ref, -1e30)
        acc_ref[...] = jnp.zeros_like(acc_ref)

    bq = q_ref.shape[1]
    units = [(j, hh, c0) for c0 in range(0, bq, ATTN_QSUB) for j in range(2) for hh in range(2)]
    n = len(units)
    nb = len(scr) // 2
    s_refs = [scr[u % nb] for u in range(n)]
    p_refs = [scr[nb + u % nb] for u in range(n)]
    first = ki == 0
    m_old, m_new = [None] * n, [None] * n

    def scores(u):
        j, hh, c0 = units[u]
        s_refs[u][...] = lax.dot_general(k_ref[0, :, hh * LANES:(hh + 1) * LANES],
                                         q_ref[0, c0:c0 + ATTN_QSUB, j * LANES:(j + 1) * LANES],
                                         (((1,), (1,)), ((), ())), preferred_element_type=f32)

    def softmax(u):
        j, hh, c0 = units[u]
        m_old[u] = m_ref[2 * j + hh, :, c0:c0 + ATTN_QSUB]
        c = jnp.where(first, 0.0, m_old[u])
        smax = jnp.max(s_refs[u][...], axis=0, keepdims=True)
        t = jnp.maximum(jnp.where(first, -1e30, 0.0), smax - c).astype(bf16)
        m_new[u] = c + t.astype(f32)
        for r0 in range(0, bk, ATTN_STRIP):
            rs = slice(r0, r0 + ATTN_STRIP)
            p_refs[u][rs, :] = jnp.exp2((s_refs[u][rs, :] - c).astype(bf16) - t)

    def update(u):
        j, hh, c0 = units[u]
        i = 2 * j + hh
        pv = jnp.dot(vt_ref[0], p_refs[u][...], preferred_element_type=f32)
        acc_ref[i, :, c0:c0 + ATTN_QSUB] = (acc_ref[i, :, c0:c0 + ATTN_QSUB]
                                            * jnp.exp2(m_old[u] - m_new[u]) + pv)
        m_ref[i, :, c0:c0 + ATTN_QSUB] = m_new[u]

    scores(0)
    for u in range(n):
        if u + 1 < n:
            scores(u + 1)
        softmax(u)
        if u > 0:
            update(u - 1)
    update(n - 1)

    @pl.when(ki == pl.num_programs(3) - 1)
    def _():
        for j in range(2):
            o = jnp.concatenate([acc_ref[i, 0:HEAD_DIM, :] * (1.0 / acc_ref[i, HEAD_DIM:HEAD_DIM + 1, :])
                                 for i in (2 * j, 2 * j + 1)], axis=0)
            o_ref[0, :, j * LANES:(j + 1) * LANES] = o.T.astype(o_ref.dtype)


def _attention(q, k, vt):
    B, T, _ = q.shape
    bq, bk = ATTN_BQ, ATTN_BK
    nb = ATTN_BUFS
    return pl.pallas_call(
        _attn_kernel,
        grid=(B, KV_HEADS, T // bq, T // bk),
        in_specs=[pl.BlockSpec((1, bq, 256), lambda b, g, qi, ki: (b, qi, g)),
                  pl.BlockSpec((1, bk, 256), lambda b, g, qi, ki: (b, ki, g)),
                  pl.BlockSpec((1, VT_ROWS, bk), lambda b, g, qi, ki: (b, g, ki))],
        out_specs=pl.BlockSpec((1, bq, 256), lambda b, g, qi, ki: (b, qi, g)),
        out_shape=jax.ShapeDtypeStruct((B, T, ATTN_DIM), bf16),
        scratch_shapes=[pltpu.VMEM((4, 1, bq), f32), pltpu.VMEM((4, VT_ROWS, bq), f32)]
        + [pltpu.VMEM((bk, ATTN_QSUB), f32)] * nb + [pltpu.VMEM((bk, ATTN_QSUB), bf16)] * nb,
        compiler_params=_cparams(("parallel", "parallel", "parallel", "arbitrary")),
        name="gqa_attention",
    )(q, k, vt)


def _back_kernel(x_ref, y_ref, bonus_ref, xg_ref, attn_ref, mod_ref, gs_ref, gup_ref, wo_ref, lng_ref,
                 lnb_ref, ng_ref, wg_ref, wu_ref, wd_ref, o_ref):
    gs = gs_ref[...]
    y = y_ref[0, 0] + y_ref[1, 0]
    mu = _head_sum(y, gs) * (1.0 / HEAD_DIM)
    yc = y - mu
    var = _head_sum(yc * yc, gs) * (1.0 / HEAD_DIM)
    yn = yc * lax.rsqrt(var + GN_EPS) * lng_ref[...] + lnb_ref[...]
    g = _dot(_sigmoid(xg_ref[0]), gup_ref[...])
    rw = ((yn + bonus_ref[0]) * g).astype(bf16)
    mix = (jnp.dot(rw, wo_ref[0:RWKV_DIM, :], preferred_element_type=f32)
           + jnp.dot(attn_ref[0], wo_ref[RWKV_DIM:, :], preferred_element_type=f32))
    x = x_ref[0] + mod_ref[0, 2:3, :] * mix

    ms = jnp.mean(x * x, axis=-1, keepdims=True)
    h = x * lax.rsqrt(ms + NORM_EPS) * ng_ref[...]
    h = (h * (1.0 + mod_ref[0, 4:5, :]) + mod_ref[0, 3:4, :]).astype(bf16)
    gt = jnp.dot(h, wg_ref[...], preferred_element_type=f32)
    up = jnp.dot(h, wu_ref[...], preferred_element_type=f32)
    act = (gt * _sigmoid(gt) * up).astype(bf16)
    o_ref[0] = x + mod_ref[0, 5:6, :] * jnp.dot(act, wd_ref[...], preferred_element_type=f32)


def _back(x, y, bonus, xg, attn, mod, gsum, p):
    B, T, _ = x.shape
    bm = FFN_ROWS
    row3 = lambda b, i: (b, i, 0)
    const2 = lambda b, i: (0, 0)
    once = pl.Buffered(1)
    cst = lambda shape, imap=const2: pl.BlockSpec(shape, imap, pipeline_mode=once)
    return pl.pallas_call(
        _back_kernel,
        grid=(B, T // bm),
        in_specs=[pl.BlockSpec((1, bm, D_MODEL), row3),
                  pl.BlockSpec((2, 1, bm, 512), lambda b, i: (0, b, i, 0)),
                  pl.BlockSpec((1, bm, 512), row3),
                  pl.BlockSpec((1, bm, LANES), row3),
                  pl.BlockSpec((1, bm, 512), row3),
                  pl.BlockSpec((1, 6, D_MODEL), lambda b, i: (b, 0, 0)),
                  cst((256, 256)), cst((128, 512)), cst((D_MODEL, D_MODEL)), cst((1, 512)), cst((1, 512)),
                  cst((1, D_MODEL)),
                  cst((D_MODEL, D_FF)), cst((D_MODEL, D_FF), lambda b, i: (0, 1)), cst((D_FF, D_MODEL))],
        out_specs=pl.BlockSpec((1, bm, D_MODEL), row3),
        out_shape=jax.ShapeDtypeStruct((B, T, D_MODEL), f32),
        compiler_params=_cparams(("parallel", "parallel"), BACK_VMEM_LIMIT),
        name="back",
    )(x, y, bonus, xg, attn, mod, gsum, p["gate_up"], p["w_out"], p["ln_g"], p["ln_b"],
      p["norm_ffn_g"], p["w_ffn_in"], p["w_ffn_in"], p["w_ffn_out"])


def _rope_tables(T):
    rows = T // GRID_W
    quarter = HEAD_DIM // 4
    freq = 1.0 / (ROPE_THETA ** (np.arange(quarter, dtype=np.float64) / quarter))
    ang_r = np.arange(rows, dtype=np.float64)[:, None] * freq[None, :]
    ang_c = np.arange(GRID_W, dtype=np.float64)[:, None] * freq[None, :]

    def table(fr, fc):
        r = jnp.broadcast_to(jnp.asarray(fr, f32)[:, None, :], (rows, GRID_W, 2 * quarter))
        c = jnp.broadcast_to(jnp.asarray(fc, f32)[None, :, :], (rows, GRID_W, 2 * quarter))
        t = jnp.concatenate([r, c], axis=-1).reshape(T, HEAD_DIM)
        return jnp.concatenate([t, t], axis=1)

    cos = table(np.concatenate([np.cos(ang_r)] * 2, axis=1), np.concatenate([np.cos(ang_c)] * 2, axis=1))
    sin = table(np.concatenate([-np.sin(ang_r), np.sin(ang_r)], axis=1),
                np.concatenate([-np.sin(ang_c), np.sin(ang_c)], axis=1))
    return cos, sin


def _block_lora(up):
    z = jnp.zeros_like(up[0])
    return jnp.concatenate([jnp.concatenate([up[0], z], axis=1),
                            jnp.concatenate([z, up[1]], axis=1)], axis=0)


def _layer_params(l, ada_w, ada_b, norm_mix_g, norm_ffn_g, w_in, conv_w, decay_w0, decay_up, iclr_a0,
                  iclr_up, gate_up, k_k, k_a, r_k, ln_x_g, ln_x_b, q_norm_g, k_norm_g, w_out,
                  w_ffn_in, w_ffn_out):
    wi = w_in[l].astype(bf16)
    wk = wi[:, 2048:2176]
    wv = wi[:, 2176:2304]
    wk2 = jnp.concatenate([wk[:, 0:64], wk[:, 0:64], wk[:, 64:128], wk[:, 64:128]], axis=1)
    w_all = jnp.concatenate([wi[:, 0:1536], wi[:, 2304:2688], wi[:, 1536:2048], wk2, wv], axis=1)
    return dict(
        ada_w=ada_w[l], ada_b=ada_b[l],
        norm_mix_g=norm_mix_g[l].reshape(1, -1), norm_ffn_g=norm_ffn_g[l].reshape(1, -1),
        w_all=w_all, conv_w=conv_w[l],
        wd=_block_lora(decay_up[l]).astype(bf16), w0=decay_w0[l].reshape(1, -1),
        wi=_block_lora(iclr_up[l]).astype(bf16), a0=iclr_a0[l].reshape(1, -1),
        gate_up=gate_up[l].astype(bf16),
        k_k=k_k[l].reshape(1, -1), k_a=k_a[l].reshape(1, -1), r_k=r_k[l].reshape(1, -1),
        ln_g=ln_x_g[l].reshape(1, -1), ln_b=ln_x_b[l].reshape(1, -1),
        qg=jnp.tile(q_norm_g[l], 8).reshape(1, -1), kg=jnp.tile(k_norm_g[l], 4).reshape(1, -1),
        w_out=w_out[l].astype(bf16), w_ffn_in=w_ffn_in[l].astype(bf16),
        w_ffn_out=w_ffn_out[l].astype(bf16))


def _layer(x, c, p, cos, sin, gsum):
    mod = _modulation(c, p["ada_w"], p["ada_b"])
    r, kr, vr, kk, lw, ic, bonus, xg, q, k, vt = _front(x, mod, p, cos, sin, gsum)
    y = _rwkv_scan(r, kr, vr, kk, lw, ic, p["k_a"])
    attn = _attention(q, k, vt)
    return _back(x, y, bonus, xg, attn, mod, gsum, p)


def kernel(x_prompt, x_sample, c_prompt, c_sample, ada_w, ada_b, norm_mix_g, norm_ffn_g, w_in, conv_w,
           decay_w0, decay_up, iclr_a0, iclr_up, gate_up, k_k, k_a, r_k, ln_x_g, ln_x_b, q_norm_g,
           k_norm_g, w_out, w_ffn_in, w_ffn_out):
    depth = ada_w.shape[0]
    params = [_layer_params(l, ada_w, ada_b, norm_mix_g, norm_ffn_g, w_in, conv_w, decay_w0, decay_up,
                            iclr_a0, iclr_up, gate_up, k_k, k_a, r_k, ln_x_g, ln_x_b, q_norm_g,
                            k_norm_g, w_out, w_ffn_in, w_ffn_out) for l in range(depth)]
    head = np.arange(256) // HEAD_DIM
    gsum = jnp.asarray(head[:, None] == head[None, :], dtype=bf16)

    def run_trunk(x, c):
        cos, sin = _rope_tables(x.shape[1])
        for p in params:
            x = _layer(x, c, p, cos, sin, gsum)
        return x

    return (run_trunk(x_prompt, c_prompt), run_trunk(x_sample, c_sample))
```

```python
import functools
import math

import jax
import jax.numpy as jnp
import numpy as np
from jax import lax
from jax.experimental import pallas as pl
from jax.experimental.pallas import tpu as pltpu

f32 = jnp.float32
bf16 = jnp.bfloat16

D_MODEL = 1024
HEAD_DIM = 64
RWKV_DIM = 512
ATTN_DIM = 512
KV_HEADS = 2
LORA = 64
D_FF = 2816
GRID_W = 64
ROPE_THETA = 10000.0
NORM_EPS = 1e-6
QK_EPS = 1e-6
GN_EPS = 64e-5
DECAY_SCALE = math.exp(-0.5)

LANES = 128
VMEM_LIMIT = 48 * 1024 * 1024
BACK_VMEM_LIMIT = 56 * 1024 * 1024

ROW_BLOCK = 512
HALO = 16
FFN_ROWS = 512
CHUNK = 64
SCAN_CHUNKS = 4
ATTN_BQ = 1024
ATTN_BK = 4096
ATTN_BUFS = 3
ATTN_QSUB = 256
ATTN_STRIP = 16
LOG2E = 1.4426950408889634
VT_ROWS = 80

C_RKV = 0
C_LORA = 1536
C_Q = 1920
C_K = 2432
C_V = 2688
C_END = 2816


def _cparams(sem, vmem_limit=None):
    return pltpu.CompilerParams(dimension_semantics=sem, vmem_limit_bytes=vmem_limit or VMEM_LIMIT)


def _dot(a, b):
    return jnp.dot(a.astype(bf16), b.astype(bf16), preferred_element_type=f32)


def _split2(a):
    hi = a.astype(bf16)
    lo = (a - hi.astype(f32)).astype(bf16)
    return hi, lo


def _head_sum(a, g):
    hi, lo = _split2(a)
    w = g.shape[0]
    return jnp.concatenate([jnp.dot(hi[:, c:c + w], g, preferred_element_type=f32)
                            + jnp.dot(lo[:, c:c + w], g, preferred_element_type=f32)
                            for c in range(0, a.shape[1], w)], axis=1)


def _sigmoid(x):
    return 1.0 / (1.0 + jnp.exp(-x))


def _iota(shape, dim):
    return lax.broadcasted_iota(jnp.int32, shape, dim)


def _mod_kernel(c_ref, w_ref, b_ref, o_ref):
    c = c_ref[...]
    s = c * _sigmoid(c)
    sh, sl = _split2(s)
    wh, wl = _split2(w_ref[...])
    acc = jnp.dot(sh, wh, preferred_element_type=f32)
    acc += jnp.dot(sh, wl, preferred_element_type=f32)
    acc += jnp.dot(sl, wh, preferred_element_type=f32)
    o_ref[...] = acc + b_ref[...]


def _modulation(c, ada_w, ada_b):
    B = c.shape[0]
    Bp = max(8, B)
    cp = jnp.pad(c, ((0, Bp - B), (0, 0)))
    n = ada_w.shape[1] // D_MODEL
    out = pl.pallas_call(
        _mod_kernel,
        grid=(n,),
        in_specs=[pl.BlockSpec((Bp, D_MODEL), lambda j: (0, 0)),
                  pl.BlockSpec((D_MODEL, D_MODEL), lambda j: (0, j)),
                  pl.BlockSpec((1, D_MODEL), lambda j: (0, j))],
        out_specs=pl.BlockSpec((Bp, D_MODEL), lambda j: (0, j)),
        out_shape=jax.ShapeDtypeStruct((Bp, n * D_MODEL), f32),
        compiler_params=_cparams(("arbitrary",)),
        name="adaln_mod",
    )(cp, ada_w, ada_b.reshape(1, -1))
    return out[:B].reshape(B, n, D_MODEL)


def _rope(x, cos, sin):
    w = x.shape[1]
    up = pltpu.roll(x, w - 16, 1)
    dn = pltpu.roll(x, 16, 1)
    first = (_iota((1, w), 1) & 16) == 0
    return x * cos + jnp.where(first, up, dn) * sin


def _qk_norm(x, g_ref, gain):
    ms = _head_sum(x * x, g_ref[...]) * (1.0 / HEAD_DIM)
    return x * lax.rsqrt(ms + QK_EPS) * gain


def _front_kernel(x_ref, xp_ref, xn_ref, mod_ref, ng_ref, w_ref, cos_ref, sin_ref, qg_ref, kg_ref, gs_ref,
                  cw_ref, wd_ref, w0_ref, wi_ref, a0_ref, kk_ref, ka_ref, rk_ref,
                  r_ref, k_ref, v_ref, kkn_ref, lw_ref, ic_ref, bonus_ref, xg_ref, q_ref, ka_out_ref, vt_ref,
                  vs_ref):
    i = pl.program_id(1)
    n = pl.num_programs(1)
    bm = x_ref.shape[1]
    halo = xp_ref.shape[1]

    def ada_norm(x):
        ms = jnp.mean(x * x, axis=-1, keepdims=True)
        h = x * lax.rsqrt(ms + NORM_EPS) * ng_ref[...]
        return h * (1.0 + mod_ref[0, 1:2, :]) + mod_ref[0, 0:1, :]

    h_cur = ada_norm(x_ref[0])
    h_ext = jnp.concatenate([ada_norm(xp_ref[0]) * jnp.where(i > 0, 1.0, 0.0), h_cur,
                             ada_norm(xn_ref[0]) * jnp.where(i < n - 1, 1.0, 0.0)], axis=0)
    rkv_ext = jnp.dot(h_ext.astype(bf16), w_ref[:, C_RKV:C_LORA], preferred_element_type=f32)
    hb = h_cur.astype(bf16)
    lora = jnp.dot(hb, w_ref[:, C_LORA:C_Q], preferred_element_type=f32)
    q = jnp.dot(hb, w_ref[:, C_Q:C_K], preferred_element_type=f32)
    k = jnp.dot(hb, w_ref[:, C_K:C_V], preferred_element_type=f32)
    vs_ref[...] = jnp.dot(hb, w_ref[:, C_V:C_END], preferred_element_type=f32)

    vt = vs_ref[...].T.astype(bf16)
    pad = jnp.where(_iota((VT_ROWS - HEAD_DIM, vt.shape[1]), 0) == 0, 1.0, 0.0).astype(bf16)
    for g in range(KV_HEADS):
        vt_ref[0, g * VT_ROWS:g * VT_ROWS + HEAD_DIM, :] = vt[g * HEAD_DIM:(g + 1) * HEAD_DIM]
        vt_ref[0, g * VT_ROWS + HEAD_DIM:(g + 1) * VT_ROWS, :] = pad
    cos = cos_ref[...]
    sin = sin_ref[...]
    cos4 = jnp.concatenate([cos] * 4, axis=1)
    sin4 = jnp.concatenate([sin] * 4, axis=1)
    qn = _rope(_qk_norm(q, gs_ref, qg_ref[...]), cos4, sin4)
    q_ref[0] = (qn * (LOG2E * HEAD_DIM ** -0.5)).astype(bf16)
    kn = _rope(_qk_norm(k, gs_ref, kg_ref[...]), cos4[:, 0:256], sin4[:, 0:256])
    half0 = _iota((1, LANES), 1) < HEAD_DIM
    ka_out_ref[0] = jnp.concatenate([jnp.where(keep, kn[:, g * LANES:(g + 1) * LANES], 0.0)
                                     for g in range(KV_HEADS) for keep in (half0, jnp.logical_not(half0))],
                                    axis=1).astype(bf16)

    next_shift = bm + 2 * halo - 1
    outs = (r_ref, k_ref, v_ref)
    rkv = []
    for p in range(3):
        cols = slice(p * 512, (p + 1) * 512)
        ext = rkv_ext[:, cols]
        y = (cw_ref[0:1, cols] * pltpu.roll(ext, 1, 0)[halo:halo + bm]
             + cw_ref[1:2, cols] * ext[halo:halo + bm]
             + cw_ref[2:3, cols] * pltpu.roll(ext, next_shift, 0)[halo:halo + bm])
        outs[p][0] = y
        rkv.append(y)
    r, kr, vr = rkv
    kkh = kr * kk_ref[...]
    kkn_ref[0] = kkh * lax.rsqrt(_head_sum(kkh * kkh, gs_ref[...]) + 1e-12)
    dl = _dot(jnp.tanh(lora[:, 0:128]), wd_ref[...]) + w0_ref[...]
    lw = -DECAY_SCALE * _sigmoid(dl)
    lw_ref[0, 0] = lw[:, 0:512]
    lw_ref[1, 0] = lw[:, 512:1024]
    ic = _sigmoid(_dot(lora[:, 128:256], wi_ref[...]) + a0_ref[...])
    ic_ref[0, 0] = ic[:, 0:512]
    ic_ref[1, 0] = ic[:, 512:1024]
    xg_ref[0] = lora[:, 256:384]
    kb = kr * (1.0 + (0.5 * (ic[:, 0:512] + ic[:, 512:1024]) - 1.0) * ka_ref[...])
    bonus_ref[0] = _head_sum(r * kb * rk_ref[...], gs_ref[...]) * vr


def _front(x, mod, p, cos, sin, gsum):
    B, T, _ = x.shape
    bm = ROW_BLOCK
    hb = bm // HALO
    nh = T // HALO
    row3 = lambda b, i: (b, i, 0)
    const2 = lambda b, i: (0, 0)
    once = pl.Buffered(1)
    cst = lambda shape: pl.BlockSpec(shape, const2, pipeline_mode=once)
    o512 = jax.ShapeDtypeStruct((B, T, 512), f32)
    o2 = jax.ShapeDtypeStruct((2, B, T, 512), f32)
    dir4 = pl.BlockSpec((2, 1, bm, 512), lambda b, i: (0, b, i, 0))
    return pl.pallas_call(
        _front_kernel,
        grid=(B, T // bm),
        in_specs=[pl.BlockSpec((1, bm, D_MODEL), row3),
                  pl.BlockSpec((1, HALO, D_MODEL), lambda b, i: (b, jnp.maximum(i * hb - 1, 0), 0)),
                  pl.BlockSpec((1, HALO, D_MODEL), lambda b, i: (b, jnp.minimum((i + 1) * hb, nh - 1), 0)),
                  pl.BlockSpec((1, 6, D_MODEL), lambda b, i: (b, 0, 0)),
                  cst((1, D_MODEL)), cst((D_MODEL, C_END)),
                  pl.BlockSpec((bm, LANES), lambda b, i: (i, 0)),
                  pl.BlockSpec((bm, LANES), lambda b, i: (i, 0)),
                  cst((1, 512)), cst((1, 256)), cst((256, 256)),
                  cst((3, 1536)), cst((128, 1024)), cst((1, 1024)), cst((128, 1024)), cst((1, 1024)),
                  cst((1, 512)), cst((1, 512)), cst((1, 512))],
        out_specs=[pl.BlockSpec((1, bm, 512), row3)] * 4 + [dir4, dir4]
        + [pl.BlockSpec((1, bm, 512), row3), pl.BlockSpec((1, bm, LANES), row3),
           pl.BlockSpec((1, bm, 512), row3), pl.BlockSpec((1, bm, 512), row3),
           pl.BlockSpec((1, KV_HEADS * VT_ROWS, bm), lambda b, i: (b, 0, i))],
        out_shape=[o512, o512, o512, o512, o2, o2, o512,
                   jax.ShapeDtypeStruct((B, T, LANES), f32),
                   jax.ShapeDtypeStruct((B, T, 512), bf16),
                   jax.ShapeDtypeStruct((B, T, 512), bf16),
                   jax.ShapeDtypeStruct((B, KV_HEADS * VT_ROWS, T), bf16)],
        scratch_shapes=[pltpu.VMEM((bm, KV_HEADS * HEAD_DIM), f32)],
        compiler_params=_cparams(("parallel", "parallel")),
        name="front",
    )(x, x, x, mod, p["norm_mix_g"], p["w_all"], cos, sin, p["qg"], p["kg"], gsum,
      p["conv_w"], p["wd"], p["w0"], p["wi"], p["a0"], p["k_k"], p["k_a"], p["r_k"])


def _scan_kernel(r_ref, k_ref, v_ref, kk_ref, lw_ref, ic_ref, ka_ref, y_ref, h_ref, *, nc):
    L = CHUNK
    d = pl.program_id(0)
    fwd = d == 0

    @pl.when(pl.program_id(2) == 0)
    def _():
        h_ref[...] = jnp.zeros_like(h_ref)

    sgn = jnp.where(fwd, 1, -1)
    col = _iota((L, LANES), 1)
    row = _iota((L, LANES), 0)
    order = ((col & (L - 1)) - row) * sgn
    left = col < L
    strict_l = (order < 0) & left
    strict_r = (order < 0) & jnp.logical_not(left)
    incl = order <= 0
    eye_r = jnp.where(col - L == row, 1.0, 0.0)
    right_f = jnp.where(left, 0.0, 1.0)
    incl_bf = jnp.where(order[:, 0:L] <= 0, 1.0, 0.0).astype(bf16)
    lane = _iota((1, LANES), 1)
    head_m = (jnp.where(lane < L, 1.0, 0.0), jnp.where(lane < L, 0.0, 1.0))
    blockdiag = (_iota((LANES, LANES), 0) < L) == (_iota((LANES, LANES), 1) < L)
    eye128 = _iota((LANES, LANES), 0) == _iota((LANES, LANES), 1)
    z128 = jnp.zeros((L, LANES), f32)
    z256 = jnp.zeros((L, 2 * LANES), f32)
    ka = ka_ref[...]
    pairs = [(i, p) for i in range(nc) for p in range(4)]
    units = [(i, p, hh) for (i, p) in pairs for hh in range(2)]

    rows_of, ch = [], []
    for i in range(nc):
        ci = jnp.where(fwd, i, nc - 1 - i)
        rows = pl.ds(pl.multiple_of(ci * L, L), L)
        rows_of.append(rows)
        lw = lw_ref[0, 0, rows, :]
        ic = ic_ref[0, 0, rows, :]
        kk = kk_ref[0, rows, :]
        kd = k_ref[0, rows, :] * (1.0 + (ic - 1.0) * ka)
        b = kk * ic
        l1 = lw.astype(bf16)
        e1 = lw - l1.astype(f32)
        l2 = e1.astype(bf16)
        l3 = (e1 - l2.astype(f32)).astype(bf16)
        cum = (jnp.dot(incl_bf, l1, preferred_element_type=f32)
               + jnp.dot(incl_bf, l2, preferred_element_type=f32)
               + jnp.dot(incl_bf, l3, preferred_element_type=f32))
        tot = jnp.where(fwd, cum[L - 1:L, :], cum[0:1, :])
        p_inv = jnp.exp(-cum)
        p_end = jnp.exp(tot - cum)
        ch.append(dict(
            v=v_ref[0, rows, :], p_tot=jnp.exp(tot),
            rt=r_ref[0, rows, :] * jnp.exp(cum), at=-kk * jnp.exp(cum - lw),
            bt=b * p_inv, kt=kd * p_inv, bh=b * p_end, kh=kd * p_end))

    def sl(i, p, name):
        return ch[i][name][:, p * LANES:(p + 1) * LANES]

    am, vm, g = {}, {}, {}
    for (i, p) in pairs:
        zt = jnp.concatenate([sl(i, p, "bt"), sl(i, p, "kt")], axis=0).T.astype(bf16)
        for hh in range(2):
            u = (i, p, hh)
            am[u] = sl(i, p, "at") * head_m[hh]
            vm[u] = sl(i, p, "v") * head_m[hh]
            xm = jnp.concatenate([am[u], sl(i, p, "rt") * head_m[hh]], axis=0)
            g[u] = _dot(xm, zt)
    pk, w1 = {}, {}
    for u in units:
        gt = g[u][0:L]
        pk[u] = jnp.where(strict_l, gt, 0.0) + eye_r
        w1[u] = _dot(jnp.where(strict_r, gt, 0.0), jnp.concatenate([z128, vm[u]], axis=0))
    for _ in range(6):
        for u in units:
            prod = _dot(pk[u], jnp.concatenate([pk[u], z128], axis=0))
            pk[u] = prod + pk[u] * right_f
    tu, ry = {}, {}
    for u in units:
        tu[u] = _dot(pk[u], jnp.concatenate([z256, jnp.concatenate([am[u], w1[u]], axis=1)], axis=0))
    for u in units:
        lhs = jnp.where(incl, g[u][L:2 * L], 0.0)
        ry[u] = _dot(lhs, jnp.concatenate([tu[u], jnp.concatenate([z128, vm[u]], axis=1)], axis=0))
    r_pair, y_pair, m_p, c_p, p_col = {}, {}, {}, {}, {}
    for (i, p) in pairs:
        u0, u1 = (i, p, 0), (i, p, 1)
        t2 = tu[u0] + tu[u1]
        r2 = ry[u0] + ry[u1]
        r_pair[i, p] = sl(i, p, "rt") + r2[:, 0:LANES]
        y_pair[i, p] = r2[:, LANES:]
        zl = jnp.concatenate([sl(i, p, "bh"), sl(i, p, "kh")], axis=0)
        zr = jnp.concatenate([t2, jnp.concatenate([z128, sl(i, p, "v")], axis=1)], axis=0)
        mc = _dot(zl.T, zr)
        m_p[i, p] = jnp.where(blockdiag, mc[:, 0:LANES], 0.0)
        c_p[i, p] = jnp.where(blockdiag, mc[:, LANES:], 0.0)
        p_col[i, p] = jnp.sum(jnp.where(eye128, sl(i, p, "p_tot"), 0.0), axis=1, keepdims=True)
    hs = [h_ref[p] for p in range(4)]
    for i in range(nc):
        for p in range(4):
            sd = _dot(jnp.concatenate([r_pair[i, p], m_p[i, p]], axis=0), hs[p])
            y_ref[0, 0, rows_of[i], p * LANES:(p + 1) * LANES] = sd[0:L] + y_pair[i, p]
            hs[p] = p_col[i, p] * hs[p] + sd[L:] + c_p[i, p]
    for p in range(4):
        h_ref[p] = hs[p]


def _rwkv_scan(r, k, v, kk, lw, ic, k_a):
    B, T, _ = r.shape
    nc = SCAN_CHUNKS
    lb = nc * CHUNK
    ns = T // lb

    def blk(d, b, s):
        return jnp.where(d == 0, s, ns - 1 - s)

    shared = pl.BlockSpec((1, lb, 512), lambda d, b, s: (b, blk(d, b, s), 0))
    perdir = pl.BlockSpec((1, 1, lb, 512), lambda d, b, s: (d, b, blk(d, b, s), 0))
    return pl.pallas_call(
        functools.partial(_scan_kernel, nc=nc),
        grid=(2, B, ns),
        in_specs=[shared, shared, shared, shared, perdir, perdir,
                  pl.BlockSpec((1, 512), lambda d, b, s: (0, 0))],
        out_specs=perdir,
        out_shape=jax.ShapeDtypeStruct((2, B, T, 512), f32),
        scratch_shapes=[pltpu.VMEM((4, LANES, LANES), f32)],
        compiler_params=_cparams(("arbitrary", "arbitrary", "arbitrary")),
        name="rwkv_scan",
    )(r, k, v, kk, lw, ic, k_a)


def _attn_kernel(q_ref, k_ref, vt_ref, o_ref, m_ref, acc_ref, *scr):
    ki = pl.program_id(3)
    bk = k_ref.shape[1]

    @pl.when(ki == 0)
    def _():
        m_ref[...] = jnp.full_like(m_ref, -1e30)
        acc_ref[...] = jnp.zeros_like(acc_ref)

    bq = q_ref.shape[1]
    units = [(j, hh, c0) for c0 in range(0, bq, ATTN_QSUB) for j in range(2) for hh in range(2)]
    n = len(units)
    nb = len(scr) // 2
    s_refs = [scr[u % nb] for u in range(n)]
    p_refs = [scr[nb + u % nb] for u in range(n)]
    first = ki == 0
    m_old, m_new = [None] * n, [None] * n

    def scores(u):
        j, hh, c0 = units[u]
        s_refs[u][...] = lax.dot_general(k_ref[0, :, hh * LANES:(hh + 1) * LANES],
                                         q_ref[0, c0:c0 + ATTN_QSUB, j * LANES:(j + 1) * LANES],
                                         (((1,), (1,)), ((), ())), preferred_element_type=f32)

    def softmax(u):
        j, hh, c0 = units[u]
        m_old[u] = m_ref[2 * j + hh, :, c0:c0 + ATTN_QSUB]
        c = jnp.where(first, 0.0, m_old[u])
        smax = jnp.max(s_refs[u][...], axis=0, keepdims=True)
        t = jnp.maximum(jnp.where(first, -1e30, 0.0), smax - c).astype(bf16)
        m_new[u] = c + t.astype(f32)
        for r0 in range(0, bk, ATTN_STRIP):
            rs = slice(r0, r0 + ATTN_STRIP)
            p_refs[u][rs, :] = jnp.exp2((s_refs[u][rs, :] - c).astype(bf16) - t)

    def update(u):
        j, hh, c0 = units[u]
        i = 2 * j + hh
        pv = jnp.dot(vt_ref[0], p_refs[u][...], preferred_element_type=f32)
        acc_ref[i, :, c0:c0 + ATTN_QSUB] = (acc_ref[i, :, c0:c0 + ATTN_QSUB]
                                            * jnp.exp2(m_old[u] - m_new[u]) + pv)
        m_ref[i, :, c0:c0 + ATTN_QSUB] = m_new[u]

    scores(0)
    for u in range(n):
        if u + 1 < n:
            scores(u + 1)
        softmax(u)
        if u > 0:
            update(u - 1)
    update(n - 1)

    @pl.when(ki == pl.num_programs(3) - 1)
    def _():
        for j in range(2):
            o = jnp.concatenate([acc_ref[i, 0:HEAD_DIM, :] * (1.0 / acc_ref[i, HEAD_DIM:HEAD_DIM + 1, :])
                                 for i in (2 * j, 2 * j + 1)], axis=0)
            o_ref[0, :, j * LANES:(j + 1) * LANES] = o.T.astype(o_ref.dtype)


def _attention(q, k, vt):
    B, T, _ = q.shape
    bq, bk = ATTN_BQ, ATTN_BK
    nb = ATTN_BUFS
    return pl.pallas_call(
        _attn_kernel,
        grid=(B, KV_HEADS, T // bq, T // bk),
        in_specs=[pl.BlockSpec((1, bq, 256), lambda b, g, qi, ki: (b, qi, g)),
                  pl.BlockSpec((1, bk, 256), lambda b, g, qi, ki: (b, ki, g)),
                  pl.BlockSpec((1, VT_ROWS, bk), lambda b, g, qi, ki: (b, g, ki))],
        out_specs=pl.BlockSpec((1, bq, 256), lambda b, g, qi, ki: (b, qi, g)),
        out_shape=jax.ShapeDtypeStruct((B, T, ATTN_DIM), bf16),
        scratch_shapes=[pltpu.VMEM((4, 1, bq), f32), pltpu.VMEM((4, VT_ROWS, bq), f32)]
        + [pltpu.VMEM((bk, ATTN_QSUB), f32)] * nb + [pltpu.VMEM((bk, ATTN_QSUB), bf16)] * nb,
        compiler_params=_cparams(("parallel", "parallel", "parallel", "arbitrary")),
        name="gqa_attention",
    )(q, k, vt)


def _back_kernel(x_ref, y_ref, bonus_ref, xg_ref, attn_ref, mod_ref, gs_ref, gup_ref, wo_ref, lng_ref,
                 lnb_ref, ng_ref, wg_ref, wu_ref, wd_ref, o_ref):
    gs = gs_ref[...]
    y = y_ref[0, 0] + y_ref[1, 0]
    mu = _head_sum(y, gs) * (1.0 / HEAD_DIM)
    yc = y - mu
    var = _head_sum(yc * yc, gs) * (1.0 / HEAD_DIM)
    yn = yc * lax.rsqrt(var + GN_EPS) * lng_ref[...] + lnb_ref[...]
    g = _dot(_sigmoid(xg_ref[0]), gup_ref[...])
    rw = ((yn + bonus_ref[0]) * g).astype(bf16)
    mix = (jnp.dot(rw, wo_ref[0:RWKV_DIM, :], preferred_element_type=f32)
           + jnp.dot(attn_ref[0], wo_ref[RWKV_DIM:, :], preferred_element_type=f32))
    x = x_ref[0] + mod_ref[0, 2:3, :] * mix

    ms = jnp.mean(x * x, axis=-1, keepdims=True)
    h = x * lax.rsqrt(ms + NORM_EPS) * ng_ref[...]
    h = (h * (1.0 + mod_ref[0, 4:5, :]) + mod_ref[0, 3:4, :]).astype(bf16)
    gt = jnp.dot(h, wg_ref[...], preferred_element_type=f32)
    up = jnp.dot(h, wu_ref[...], preferred_element_type=f32)
    act = (gt * _sigmoid(gt) * up).astype(bf16)
    o_ref[0] = x + mod_ref[0, 5:6, :] * jnp.dot(act, wd_ref[...], preferred_element_type=f32)


def _back(x, y, bonus, xg, attn, mod, gsum, p):
    B, T, _ = x.shape
    bm = FFN_ROWS
    row3 = lambda b, i: (b, i, 0)
    const2 = lambda b, i: (0, 0)
    once = pl.Buffered(1)
    cst = lambda shape, imap=const2: pl.BlockSpec(shape, imap, pipeline_mode=once)
    return pl.pallas_call(
        _back_kernel,
        grid=(B, T // bm),
        in_specs=[pl.BlockSpec((1, bm, D_MODEL), row3),
                  pl.BlockSpec((2, 1, bm, 512), lambda b, i: (0, b, i, 0)),
                  pl.BlockSpec((1, bm, 512), row3),
                  pl.BlockSpec((1, bm, LANES), row3),
                  pl.BlockSpec((1, bm, 512), row3),
                  pl.BlockSpec((1, 6, D_MODEL), lambda b, i: (b, 0, 0)),
                  cst((256, 256)), cst((128, 512)), cst((D_MODEL, D_MODEL)), cst((1, 512)), cst((1, 512)),
                  cst((1, D_MODEL)),
                  cst((D_MODEL, D_FF)), cst((D_MODEL, D_FF), lambda b, i: (0, 1)), cst((D_FF, D_MODEL))],
        out_specs=pl.BlockSpec((1, bm, D_MODEL), row3),
        out_shape=jax.ShapeDtypeStruct((B, T, D_MODEL), f32),
        compiler_params=_cparams(("parallel", "parallel"), BACK_VMEM_LIMIT),
        name="back",
    )(x, y, bonus, xg, attn, mod, gsum, p["gate_up"], p["w_out"], p["ln_g"], p["ln_b"],
      p["norm_ffn_g"], p["w_ffn_in"], p["w_ffn_in"], p["w_ffn_out"])


def _rope_tables(T):
    rows = T // GRID_W
    quarter = HEAD_DIM // 4
    freq = 1.0 / (ROPE_THETA ** (np.arange(quarter, dtype=np.float64) / quarter))
    ang_r = np.arange(rows, dtype=np.float64)[:, None] * freq[None, :]
    ang_c = np.arange(GRID_W, dtype=np.float64)[:, None] * freq[None, :]

    def table(fr, fc):
        r = jnp.broadcast_to(jnp.asarray(fr, f32)[:, None, :], (rows, GRID_W, 2 * quarter))
        c = jnp.broadcast_to(jnp.asarray(fc, f32)[None, :, :], (rows, GRID_W, 2 * quarter))
        t = jnp.concatenate([r, c], axis=-1).reshape(T, HEAD_DIM)
        return jnp.concatenate([t, t], axis=1)

    cos = table(np.concatenate([np.cos(ang_r)] * 2, axis=1), np.concatenate([np.cos(ang_c)] * 2, axis=1))
    sin = table(np.concatenate([-np.sin(ang_r), np.sin(ang_r)], axis=1),
                np.concatenate([-np.sin(ang_c), np.sin(ang_c)], axis=1))
    return cos, sin


def _block_lora(up):
    z = jnp.zeros_like(up[0])
    return jnp.concatenate([jnp.concatenate([up[0], z], axis=1),
                            jnp.concatenate([z, up[1]], axis=1)], axis=0)


def _layer_params(l, ada_w, ada_b, norm_mix_g, norm_ffn_g, w_in, conv_w, decay_w0, decay_up, iclr_a0,
                  iclr_up, gate_up, k_k, k_a, r_k, ln_x_g, ln_x_b, q_norm_g, k_norm_g, w_out,
                  w_ffn_in, w_ffn_out):
    wi = w_in[l].astype(bf16)
    wk = wi[:, 2048:2176]
    wv = wi[:, 2176:2304]
    wk2 = jnp.concatenate([wk[:, 0:64], wk[:, 0:64], wk[:, 64:128], wk[:, 64:128]], axis=1)
    w_all = jnp.concatenate([wi[:, 0:1536], wi[:, 2304:2688], wi[:, 1536:2048], wk2, wv], axis=1)
    return dict(
        ada_w=ada_w[l], ada_b=ada_b[l],
        norm_mix_g=norm_mix_g[l].reshape(1, -1), norm_ffn_g=norm_ffn_g[l].reshape(1, -1),
        w_all=w_all, conv_w=conv_w[l],
        wd=_block_lora(decay_up[l]).astype(bf16), w0=decay_w0[l].reshape(1, -1),
        wi=_block_lora(iclr_up[l]).astype(bf16), a0=iclr_a0[l].reshape(1, -1),
        gate_up=gate_up[l].astype(bf16),
        k_k=k_k[l].reshape(1, -1), k_a=k_a[l].reshape(1, -1), r_k=r_k[l].reshape(1, -1),
        ln_g=ln_x_g[l].reshape(1, -1), ln_b=ln_x_b[l].reshape(1, -1),
        qg=jnp.tile(q_norm_g[l], 8).reshape(1, -1), kg=jnp.tile(k_norm_g[l], 4).reshape(1, -1),
        w_out=w_out[l].astype(bf16), w_ffn_in=w_ffn_in[l].astype(bf16),
        w_ffn_out=w_ffn_out[l].astype(bf16))


def _layer(x, c, p, cos, sin, gsum):
    mod = _modulation(c, p["ada_w"], p["ada_b"])
    r, kr, vr, kk, lw, ic, bonus, xg, q, k, vt = _front(x, mod, p, cos, sin, gsum)
    y = _rwkv_scan(r, kr, vr, kk, lw, ic, p["k_a"])
    attn = _attention(q, k, vt)
    return _back(x, y, bonus, xg, attn, mod, gsum, p)


def kernel(x_prompt, x_sample, c_prompt, c_sample, ada_w, ada_b, norm_mix_g, norm_ffn_g, w_in, conv_w,
           decay_w0, decay_up, iclr_a0, iclr_up, gate_up, k_k, k_a, r_k, ln_x_g, ln_x_b, q_norm_g,
           k_norm_g, w_out, w_ffn_in, w_ffn_out):
    depth = ada_w.shape[0]
    params = [_layer_params(l, ada_w, ada_b, norm_mix_g, norm_ffn_g, w_in, conv_w, decay_w0, decay_up,
                            iclr_a0, iclr_up, gate_up, k_k, k_a, r_k, ln_x_g, ln_x_b, q_norm_g,
                            k_norm_g, w_out, w_ffn_in, w_ffn_out) for l in range(depth)]
    head = np.arange(256) // HEAD_DIM
    gsum = jnp.asarray(head[:, None] == head[None, :], dtype=bf16)

    def run_trunk(x, c):
        cos, sin = _rope_tables(x.shape[1])
        for p in params:
            x = _layer(x, c, p, cos, sin, gsum)
        return x

    return (run_trunk(x_prompt, c_prompt), run_trunk(x_sample, c_sample))
```
